```python
import jax, jax.numpy as jnp
from jax import lax
import numpy as np

D_MODEL = 4096
BATCH = 2
SEQ = 8192
DEPTH = 2

HEAD_DIM = 128
BLOCK = 128
ROPE_THETA = 10000.0
NORM_EPS = 1e-6
A_HEADS = D_MODEL // HEAD_DIM // 2
A_KV_HEADS = max(1, A_HEADS // 8)
WINDOW = 128
B_HEADS = D_MODEL // HEAD_DIM // 2
C_HEADS = D_MODEL // HEAD_DIM
C_KV_HEADS = C_HEADS // 4
IDX_HEADS = 32
IDX_DIM = 128
TOPK_MAX = 256
D_FF = 256 * ((8 * D_MODEL // 3 + 255) // 256)
CONV_WIDTH = 3

EVEN_IN = (A_HEADS + 2 * A_KV_HEADS) * HEAD_DIM + 3 * B_HEADS * HEAD_DIM
ODD_IN = (C_HEADS + 2 * C_KV_HEADS) * HEAD_DIM + IDX_HEADS * IDX_DIM + IDX_DIM + IDX_HEADS
N_EVEN = (DEPTH + 1) // 2
N_ODD = DEPTH // 2

kernel_name = 'hybrid_swa_stickbreak_dsa_convffn_adaln'


def _split(a, sizes):
    offs = np.cumsum(sizes)[:-1].tolist()
    return jnp.split(a, offs, axis=-1)


def rms_norm(x, g):
    x32 = x.astype(jnp.float32)
    y = x32 * lax.rsqrt(jnp.mean(x32 * x32, axis=-1, keepdims=True) + NORM_EPS)
    return (y * g.astype(jnp.float32)).astype(x.dtype)


def rope(x, pos):
    d = x.shape[-1]
    inv_freq = ROPE_THETA ** (-jnp.arange(0, d, 2, dtype=jnp.float32) / d)
    ang = pos.astype(jnp.float32)[..., None] * inv_freq
    cos = jnp.cos(ang)[:, :, None, :]
    sin = jnp.sin(ang)[:, :, None, :]
    x32 = x.astype(jnp.float32)
    x1, x2 = x32[..., : d // 2], x32[..., d // 2:]
    return jnp.concatenate([x1 * cos - x2 * sin, x2 * cos + x1 * sin], axis=-1).astype(x.dtype)


def sliding_window_sink_attention(q, k, v, sinks):
    B, T, H, d = q.shape
    Hkv = k.shape[2]
    G = H // Hkv
    n = T // BLOCK
    qb = q.reshape(B, n, BLOCK, Hkv, G, d)

    def with_prev(a):
        a = a.reshape(B, n, BLOCK, Hkv, d)
        prev = jnp.concatenate([jnp.zeros_like(a[:, :1]), a[:, :-1]], axis=1)
        return jnp.concatenate([prev, a], axis=2)

    kw, vw = with_prev(k), with_prev(v)
    s = jnp.einsum('bnqhgd,bnkhd->bnhgqk', qb, kw, preferred_element_type=jnp.float32) * (d ** -0.5)
    qi = jnp.arange(BLOCK)[:, None]
    kj = jnp.arange(2 * BLOCK)[None, :]
    rel = qi + BLOCK - kj
    band = (rel >= 0) & (rel < WINDOW)
    not_before_start = (jnp.arange(n)[:, None, None] > 0) | (kj >= BLOCK)[None]
    mask = band[None] & not_before_start
    s = jnp.where(mask[None, :, None, None], s, -jnp.inf)
    sink = sinks.astype(jnp.float32).reshape(Hkv, G)[None, None, :, :, None, None]
    m = jnp.maximum(jnp.max(s, axis=-1, keepdims=True), sink)
    p = jnp.exp(s - m)
    p = p / (jnp.sum(p, axis=-1, keepdims=True) + jnp.exp(sink - m))
    o = jnp.einsum('bnhgqk,bnkhd->bnqhgd', p.astype(v.dtype), vw)
    return o.reshape(B, T, H * d)


def stick_breaking_attention(q, k, v):
    B, T, H, d = q.shape
    outs = []
    for i in range(T // BLOCK):
        start, end = i * BLOCK, (i + 1) * BLOCK
        z = jnp.einsum('bqhd,bkhd->bhqk', q[:, start:end], k[:, :end],
                       preferred_element_type=jnp.float32) * (d ** -0.5)
        t_idx = start + jnp.arange(BLOCK)
        strict = jnp.arange(end)[None, :] < t_idx[:, None]
        log_keep = jnp.where(strict, jax.nn.log_sigmoid(-z), 0.0)
        after = lax.cumsum(log_keep, axis=3, reverse=True) - log_keep
        w = jnp.where(strict, jnp.exp(jax.nn.log_sigmoid(z) + after), 0.0)
        outs.append(jnp.einsum('bhqk,bkhd->bqhd', w.astype(v.dtype), v[:, :end]))
    return jnp.concatenate(outs, axis=1).reshape(B, T, H * d)


def dsa_attention(q, k, v, qi, ki, wi):
    B, T, H, d = q.shape
    Hkv = k.shape[2]
    G = H // Hkv
    top_k = min(TOPK_MAX, T // 4)
    idx_scale = (IDX_HEADS ** -0.5) * (IDX_DIM ** -0.5)
    gather = jax.vmap(lambda a, ix: a[ix])
    outs = []
    for i in range(T // BLOCK):
        start, end = i * BLOCK, (i + 1) * BLOCK
        L = max(end, top_k)
        t_idx = start + jnp.arange(BLOCK)
        logits = jnp.einsum('bqhd,bkd->bqhk', qi[:, start:end], ki[:, :L],
                            preferred_element_type=jnp.float32)
        score = jnp.einsum('bqhk,bqh->bqk', jax.nn.relu(logits),
                           wi[:, start:end].astype(jnp.float32)) * idx_scale
        score = jnp.where(jnp.arange(L)[None, None, :] <= t_idx[None, :, None], score, -jnp.inf)
        _, idx = lax.top_k(score, top_k)
        valid = idx <= t_idx[None, :, None]
        kg = gather(k, idx)
        vg = gather(v, idx)
        qg = q[:, start:end].reshape(B, BLOCK, Hkv, G, d)
        s = jnp.einsum('bqhgd,bqkhd->bqhgk', qg, kg, preferred_element_type=jnp.float32) * (d ** -0.5)
        s = jnp.where(valid[:, :, None, None, :], s, -jnp.inf)
        p = jax.nn.softmax(s, axis=-1)
        o = jnp.einsum('bqhgk,bqkhd->bqhgd', p.astype(v.dtype), vg)
        outs.append(o.reshape(B, BLOCK, H * d))
    return jnp.concatenate(outs, axis=1)


def even_mixer(h, pos, w_in, sinks, w_o):
    B, T, _ = h.shape
    qa, ka, va, qb, kb, vb = _split(h @ w_in, [A_HEADS * HEAD_DIM, A_KV_HEADS * HEAD_DIM,
                                               A_KV_HEADS * HEAD_DIM, B_HEADS * HEAD_DIM,
                                               B_HEADS * HEAD_DIM, B_HEADS * HEAD_DIM])
    qa = rope(qa.reshape(B, T, A_HEADS, HEAD_DIM), pos)
    ka = rope(ka.reshape(B, T, A_KV_HEADS, HEAD_DIM), pos)
    oa = sliding_window_sink_attention(qa, ka, va.reshape(B, T, A_KV_HEADS, HEAD_DIM), sinks)
    ob = stick_breaking_attention(qb.reshape(B, T, B_HEADS, HEAD_DIM),
                                  kb.reshape(B, T, B_HEADS, HEAD_DIM),
                                  vb.reshape(B, T, B_HEADS, HEAD_DIM))
    return jnp.concatenate([oa, ob], axis=-1) @ w_o


def odd_mixer(h, pos, w_in, idx_k_g, w_o):
    B, T, _ = h.shape
    q, k, v, qi, ki, wi = _split(h @ w_in, [C_HEADS * HEAD_DIM, C_KV_HEADS * HEAD_DIM,
                                            C_KV_HEADS * HEAD_DIM, IDX_HEADS * IDX_DIM,
                                            IDX_DIM, IDX_HEADS])
    q = rope(q.reshape(B, T, C_HEADS, HEAD_DIM), pos)
    k = rope(k.reshape(B, T, C_KV_HEADS, HEAD_DIM), pos)
    qi = rope(qi.reshape(B, T, IDX_HEADS, IDX_DIM), pos)
    ki = rope(rms_norm(ki, idx_k_g)[:, :, None, :], pos)[:, :, 0, :]
    o = dsa_attention(q, k, v.reshape(B, T, C_KV_HEADS, HEAD_DIM), qi, ki, wi)
    return o @ w_o


def conv_gated_ffn(h, w_gate, w_up, conv_w, conv_b, w_down):
    T = h.shape[1]
    g = h @ w_gate
    gp = jnp.pad(g, ((0, 0), (CONV_WIDTH - 1, 0), (0, 0)))
    g = sum(conv_w[j] * gp[:, j:j + T] for j in range(CONV_WIDTH)) + conv_b
    return (jax.nn.silu(g) * (h @ w_up)) @ w_down


def setup_inputs(seed: int = 0) -> dict:
    key = jax.random.key(seed)
    ks = jax.random.split(key, 24)

    def nrm(k, shape, scale):
        return jax.random.normal(k, shape, jnp.float32) * scale

    x = nrm(ks[0], (BATCH, SEQ, D_MODEL), 1.0)
    c = nrm(ks[1], (BATCH, D_MODEL), 1.0)
    offsets = jax.random.randint(ks[2], (BATCH, 1), 0, 4096, dtype=jnp.int32)
    positions = offsets + jnp.arange(SEQ, dtype=jnp.int32)[None, :]
    return {
        'x': x,
        'c': c,
        'positions': positions,
        'norm1_g': 1.0 + nrm(ks[3], (DEPTH, D_MODEL), 0.02),
        'norm2_g': 1.0 + nrm(ks[4], (DEPTH, D_MODEL), 0.02),
        'ada_w': nrm(ks[5], (DEPTH, D_MODEL, 6 * D_MODEL), 0.5 * D_MODEL ** -0.5),
        'ada_b': nrm(ks[6], (DEPTH, 6 * D_MODEL), 0.01),
        'even_w_in': nrm(ks[7], (N_EVEN, D_MODEL, EVEN_IN), D_MODEL ** -0.5),
        'even_sinks': nrm(ks[8], (N_EVEN, A_HEADS), 1.0),
        'even_w_o': nrm(ks[9], (N_EVEN, (A_HEADS + B_HEADS) * HEAD_DIM, D_MODEL),
                        ((A_HEADS + B_HEADS) * HEAD_DIM) ** -0.5),
        'odd_w_in': nrm(ks[10], (N_ODD, D_MODEL, ODD_IN), D_MODEL ** -0.5),
        'odd_idx_k_g': 1.0 + nrm(ks[11], (N_ODD, IDX_DIM), 0.02),
        'odd_w_o': nrm(ks[12], (N_ODD, C_HEADS * HEAD_DIM, D_MODEL), (C_HEADS * HEAD_DIM) ** -0.5),
        'ffn_w_gate': nrm(ks[13], (DEPTH, D_MODEL, D_FF), D_MODEL ** -0.5),
        'ffn_w_up': nrm(ks[14], (DEPTH, D_MODEL, D_FF), D_MODEL ** -0.5),
        'ffn_conv_w': nrm(ks[15], (DEPTH, CONV_WIDTH, D_FF), CONV_WIDTH ** -0.5),
        'ffn_conv_b': nrm(ks[16], (DEPTH, D_FF), 0.01),
        'ffn_w_down': nrm(ks[17], (DEPTH, D_FF, D_MODEL), D_FF ** -0.5),
        'final_g': 1.0 + nrm(ks[18], (D_MODEL,), 0.02),
    }


def reference(x, c, positions, norm1_g, norm2_g, ada_w, ada_b, even_w_in, even_sinks, even_w_o,
              odd_w_in, odd_idx_k_g, odd_w_o, ffn_w_gate, ffn_w_up, ffn_conv_w, ffn_conv_b,
              ffn_w_down, final_g):
    h = x
    c_act = jax.nn.silu(c)
    for layer in range(DEPTH):
        mod = c_act @ ada_w[layer] + ada_b[layer]
        shift1, scale1, gate1, shift2, scale2, gate2 = [m[:, None, :] for m in jnp.split(mod, 6, axis=-1)]
        hn = rms_norm(h, norm1_g[layer]) * (1.0 + scale1) + shift1
        if layer % 2 == 0:
            j = layer // 2
            mix = even_mixer(hn, positions, even_w_in[j], even_sinks[j], even_w_o[j])
        else:
            j = layer // 2
            mix = odd_mixer(hn, positions, odd_w_in[j], odd_idx_k_g[j], odd_w_o[j])
        h = h + gate1 * mix
        hn = rms_norm(h, norm2_g[layer]) * (1.0 + scale2) + shift2
        h = h + gate2 * conv_gated_ffn(hn, ffn_w_gate[layer], ffn_w_up[layer], ffn_conv_w[layer],
                                       ffn_conv_b[layer], ffn_w_down[layer])
    return rms_norm(h, final_g)
```

```python
import functools

import jax
import jax.numpy as jnp
from jax import lax
from jax.experimental import pallas as pl
from jax.experimental.pallas import tpu as pltpu

F32 = jnp.float32
BF16 = jnp.bfloat16

HEAD_DIM = 128
BLOCK = 128
ROPE_THETA = 10000.0
NORM_EPS = 1e-6
TOPK_MAX = 256
NEG = -1e30
INT_MIN = -(2 ** 31)
VMEM_LIMIT_BYTES = 56 * 1024 * 1024


def _params(*sem):
    return pltpu.CompilerParams(dimension_semantics=sem, vmem_limit_bytes=VMEM_LIMIT_BYTES)


def _dot(a, b):
    return jnp.dot(a, b, preferred_element_type=F32)


def _dot_nt(a, b):
    return lax.dot_general(a, b, (((1,), (1,)), ((), ())), preferred_element_type=F32)


def _rope(x, cos2, sin2):
    return x * cos2 + pltpu.roll(x, HEAD_DIM // 2, axis=1) * sin2


def _ada_kernel(c_ref, w_ref, b_ref, o_ref):
    c = c_ref[...]
    ca = (c * jax.nn.sigmoid(c)).astype(BF16)
    o_ref[0] = _dot(ca, w_ref[0].astype(BF16)) + b_ref[0]


def _ada(c_pad, ada_w, ada_b, tn=512):
    n_layers, d, n = ada_w.shape
    rows = c_pad.shape[0]
    return pl.pallas_call(
        _ada_kernel,
        grid=(n_layers, n // tn),
        in_specs=[
            pl.BlockSpec((rows, d), lambda l, j: (0, 0)),
            pl.BlockSpec((1, d, tn), lambda l, j: (l, 0, j)),
            pl.BlockSpec((1, 1, tn), lambda l, j: (l, 0, j)),
        ],
        out_specs=pl.BlockSpec((1, rows, tn), lambda l, j: (l, 0, j)),
        out_shape=jax.ShapeDtypeStruct((n_layers, rows, n), F32),
        compiler_params=_params("arbitrary", "arbitrary"),
        name="ada_mod",
    )(c_pad, ada_w, ada_b.reshape(n_layers, 1, n))


def _norm_mod_kernel(x_ref, g_ref, sc_ref, sh_ref, o_ref):
    x = x_ref[...]
    y = x * lax.rsqrt(jnp.mean(x * x, axis=-1, keepdims=True) + NORM_EPS) * g_ref[...]
    o_ref[...] = (y * (1.0 + sc_ref[0]) + sh_ref[0]).astype(o_ref.dtype)


def _norm_plain_kernel(x_ref, g_ref, o_ref):
    x = x_ref[...]
    y = x * lax.rsqrt(jnp.mean(x * x, axis=-1, keepdims=True) + NORM_EPS) * g_ref[...]
    o_ref[...] = y.astype(o_ref.dtype)


def _norm_mod(h, g, scale, shift, seq, tr=256):
    nt, d = h.shape
    tr = min(tr, seq)
    bps = seq // tr
    return pl.pallas_call(
        _norm_mod_kernel,
        grid=(nt // tr,),
        in_specs=[
            pl.BlockSpec((tr, d), lambda i: (i, 0)),
            pl.BlockSpec((1, d), lambda i: (0, 0)),
            pl.BlockSpec((1, 1, d), lambda i: (i // bps, 0, 0)),
            pl.BlockSpec((1, 1, d), lambda i: (i // bps, 0, 0)),
        ],
        out_specs=pl.BlockSpec((tr, d), lambda i: (i, 0)),
        out_shape=jax.ShapeDtypeStruct((nt, d), BF16),
        compiler_params=_params("arbitrary"),
        name="norm_mod",
    )(h, g.reshape(1, d), scale, shift)


def _norm_plain(h, g, tr=256):
    nt, d = h.shape
    tr = min(tr, nt)
    return pl.pallas_call(
        _norm_plain_kernel,
        grid=(nt // tr,),
        in_specs=[pl.BlockSpec((tr, d), lambda i: (i, 0)), pl.BlockSpec((1, d), lambda i: (0, 0))],
        out_specs=pl.BlockSpec((tr, d), lambda i: (i, 0)),
        out_shape=jax.ShapeDtypeStruct((nt, d), F32),
        compiler_params=_params("arbitrary"),
        name="norm_final",
    )(h, g.reshape(1, d))


def _mm_kernel(*refs, n_parts, has_res):
    acc = None
    for x_ref, w_ref in zip(refs[:n_parts], refs[n_parts:2 * n_parts]):
        d = _dot(x_ref[...], w_ref[...])
        acc = d if acc is None else acc + d
    rest = refs[2 * n_parts:]
    if has_res:
        res_ref, gate_ref, o_ref = rest
        o_ref[...] = res_ref[...] + gate_ref[0] * acc
    else:
        (o_ref,) = rest
        o_ref[...] = acc.astype(o_ref.dtype)


def _matmul(xs, w, n_out, col_off, tm, tn, out_dtype, name, res=None, gate=None, seq=None):
    m, kp = xs[0].shape
    tm = min(tm, m if seq is None else seq)
    tn = min(tn, n_out)
    assert m % tm == 0 and n_out % tn == 0 and col_off % tn == 0
    cb = col_off // tn
    n_parts = len(xs)
    in_specs = [pl.BlockSpec((tm, kp), lambda i, j: (i, 0)) for _ in xs]
    in_specs += [pl.BlockSpec((kp, tn), lambda i, j, p=p: (p, j + cb)) for p in range(n_parts)]
    args = list(xs) + [w] * n_parts
    if res is not None:
        bps = seq // tm
        in_specs += [pl.BlockSpec((tm, tn), lambda i, j: (i, j)),
                     pl.BlockSpec((1, 1, tn), lambda i, j: (i // bps, 0, j))]
        args += [res, gate]
    return pl.pallas_call(
        functools.partial(_mm_kernel, n_parts=n_parts, has_res=res is not None),
        grid=(m // tm, n_out // tn),
        in_specs=in_specs,
        out_specs=pl.BlockSpec((tm, tn), lambda i, j: (i, j)),
        out_shape=jax.ShapeDtypeStruct((m, n_out), out_dtype),
        compiler_params=_params("arbitrary", "arbitrary"),
        name=name,
    )(*args)


HALO = 16


def _ffn_up_kernel(x_ref, xh_ref, wg_ref, wu_ref, cw_ref, cb_ref, o_ref, gs_ref, *, tm, bps):
    i = pl.program_id(0)
    x = x_ref[...]
    wg = wg_ref[...]
    g = _dot(x, wg)
    u = _dot(x, wu_ref[...])
    gh = _dot(xh_ref[...], wg)
    gs_ref[0:HALO, :] = jnp.where(i % bps == 0, 0.0, gh)
    gs_ref[HALO:, :] = g
    g1 = gs_ref[HALO - 1:HALO - 1 + tm, :]
    g2 = gs_ref[HALO - 2:HALO - 2 + tm, :]
    cw = cw_ref[...]
    gc = cw[0:1] * g2 + cw[1:2] * g1 + cw[2:3] * g + cb_ref[...]
    o_ref[...] = (gc * jax.nn.sigmoid(gc) * u).astype(o_ref.dtype)


def _ffn_up(hn, wg, wu, conv_w, conv_b, seq, tm=1024, tn=256):
    m, d = hn.shape
    dff = wg.shape[1]
    tm = min(tm, seq)
    assert dff % tn == 0 and tm % HALO == 0
    bps = seq // tm
    hpb = tm // HALO
    return pl.pallas_call(
        functools.partial(_ffn_up_kernel, tm=tm, bps=bps),
        grid=(m // tm, dff // tn),
        in_specs=[
            pl.BlockSpec((tm, d), lambda i, j: (i, 0)),
            pl.BlockSpec((HALO, d), lambda i, j: (jnp.maximum(i * hpb - 1, 0), 0)),
            pl.BlockSpec((d, tn), lambda i, j: (0, j)),
            pl.BlockSpec((d, tn), lambda i, j: (0, j)),
            pl.BlockSpec((3, tn), lambda i, j: (0, j)),
            pl.BlockSpec((1, tn), lambda i, j: (0, j)),
        ],
        out_specs=pl.BlockSpec((tm, tn), lambda i, j: (i, j)),
        out_shape=jax.ShapeDtypeStruct((m, dff), BF16),
        scratch_shapes=[pltpu.VMEM((tm + HALO, tn), F32)],
        compiler_params=_params("arbitrary", "arbitrary"),
        name="ffn_up",
    )(hn, hn, wg, wu, conv_w, conv_b.reshape(1, dff))


def _swa_kernel(sink_ref, q_ref, kc_ref, kp_ref, vc_ref, vp_ref, cq_ref, sq_ref, cp_ref, sp_ref,
                o_ref, *, n_heads, n_kv, scale):
    i = pl.program_id(1)
    cq, sq, cp, sp = cq_ref[...], sq_ref[...], cp_ref[...], sp_ref[...]
    group = n_heads // n_kv
    qi = lax.broadcasted_iota(jnp.int32, (BLOCK, BLOCK), 0)
    kj = lax.broadcasted_iota(jnp.int32, (BLOCK, BLOCK), 1)
    mask_c = kj <= qi
    mask_p = (kj > qi) & (i > 0)
    for hk in range(n_kv):
        sl = slice(hk * HEAD_DIM, (hk + 1) * HEAD_DIM)
        kc = _rope(kc_ref[:, sl].astype(F32), cq, sq).astype(BF16)
        kp = _rope(kp_ref[:, sl].astype(F32), cp, sp).astype(BF16)
        vc = vc_ref[:, sl]
        vp = vp_ref[:, sl]
        for g in range(group):
            h = hk * group + g
            hs = slice(h * HEAD_DIM, (h + 1) * HEAD_DIM)
            q = _rope(q_ref[:, hs].astype(F32), cq, sq).astype(BF16)
            s_c = jnp.where(mask_c, _dot_nt(q, kc) * scale, NEG)
            s_p = jnp.where(mask_p, _dot_nt(q, kp) * scale, NEG)
            sink = sink_ref[h]
            m = jnp.maximum(jnp.max(s_c, axis=-1, keepdims=True), jnp.max(s_p, axis=-1, keepdims=True))
            m = jnp.maximum(m, sink)
            p_c = jnp.exp(s_c - m)
            p_p = jnp.exp(s_p - m)
            den = (jnp.sum(p_c, axis=-1, keepdims=True) + jnp.sum(p_p, axis=-1, keepdims=True)
                   + jnp.exp(sink - m))
            o = _dot(p_c.astype(BF16), vc) + _dot(p_p.astype(BF16), vp)
            o_ref[:, hs] = (o / den).astype(o_ref.dtype)


def _swa(proj, cos2, sin2, sinks, batch, seq, n_heads, n_kv):
    nt = proj.shape[0]
    nq = seq // BLOCK
    qw, kw = n_heads * HEAD_DIM, n_kv * HEAD_DIM
    kcol = n_heads // n_kv
    vcol = kcol + 1
    cur = lambda b, i: b * nq + i
    prev = lambda b, i: b * nq + jnp.maximum(i - 1, 0)
    return pl.pallas_call(
        functools.partial(_swa_kernel, n_heads=n_heads, n_kv=n_kv, scale=HEAD_DIM ** -0.5),
        grid=(batch, nq),
        in_specs=[
            pl.BlockSpec(memory_space=pltpu.SMEM),
            pl.BlockSpec((BLOCK, qw), lambda b, i: (cur(b, i), 0)),
            pl.BlockSpec((BLOCK, kw), lambda b, i: (cur(b, i), kcol)),
            pl.BlockSpec((BLOCK, kw), lambda b, i: (prev(b, i), kcol)),
            pl.BlockSpec((BLOCK, kw), lambda b, i: (cur(b, i), vcol)),
            pl.BlockSpec((BLOCK, kw), lambda b, i: (prev(b, i), vcol)),
            pl.BlockSpec((BLOCK, HEAD_DIM), lambda b, i: (cur(b, i), 0)),
            pl.BlockSpec((BLOCK, HEAD_DIM), lambda b, i: (cur(b, i), 0)),
            pl.BlockSpec((BLOCK, HEAD_DIM), lambda b, i: (prev(b, i), 0)),
            pl.BlockSpec((BLOCK, HEAD_DIM), lambda b, i: (prev(b, i), 0)),
        ],
        out_specs=pl.BlockSpec((BLOCK, qw), lambda b, i: (cur(b, i), 0)),
        out_shape=jax.ShapeDtypeStruct((nt, qw), BF16),
        compiler_params=_params("arbitrary", "arbitrary"),
        name="swa_attn",
    )(sinks, proj, proj, proj, proj, proj, cos2, sin2, cos2, sin2)


def _sb_kernel(q_ref, k_ref, v_ref, o_ref, *, scale):
    i = pl.program_id(2)
    q = q_ref[...]
    row = lax.broadcasted_iota(jnp.int32, (BLOCK, BLOCK), 0)
    col = lax.broadcasted_iota(jnp.int32, (BLOCK, BLOCK), 1)
    later = (row > col).astype(BF16)
    strict = col < row

    def step(j, carry, acc, diag):
        off = pl.multiple_of(j * BLOCK, BLOCK)
        k = k_ref[pl.ds(off, BLOCK), :]
        v = v_ref[pl.ds(off, BLOCK), :]
        z = _dot_nt(q, k) * scale
        soft = jnp.log1p(jnp.exp(-jnp.abs(z)))
        log_beta = jnp.minimum(z, 0.0) - soft
        log_keep = jnp.minimum(-z, 0.0) - soft
        if diag:
            log_keep = jnp.where(strict, log_keep, 0.0)
        hi = log_keep.astype(BF16)
        lo = (log_keep - hi.astype(F32)).astype(BF16)
        after = _dot(hi, later) + _dot(lo, later) + carry
        w = jnp.exp(log_beta + after)
        if diag:
            w = jnp.where(strict, w, 0.0)
        acc = acc + _dot(w.astype(BF16), v)
        carry = carry + jnp.sum(log_keep, axis=-1, keepdims=True)
        return carry, acc

    carry, acc = step(i, jnp.zeros((BLOCK, 1), F32), jnp.zeros((BLOCK, HEAD_DIM), F32), True)

    def body(t, ca):
        return step(i - 1 - t, ca[0], ca[1], False)

    carry, acc = lax.fori_loop(0, i, body, (carry, acc))
    o_ref[...] = acc.astype(o_ref.dtype)


def _sb(proj, batch, seq, n_heads, qcol, kcol, vcol):
    nt = proj.shape[0]
    nq = seq // BLOCK
    return pl.pallas_call(
        functools.partial(_sb_kernel, scale=HEAD_DIM ** -0.5),
        grid=(batch, n_heads, nq),
        in_specs=[
            pl.BlockSpec((BLOCK, HEAD_DIM), lambda b, h, i: (b * nq + i, qcol + h)),
            pl.BlockSpec((seq, HEAD_DIM), lambda b, h, i: (b, kcol + h)),
            pl.BlockSpec((seq, HEAD_DIM), lambda b, h, i: (b, vcol + h)),
        ],
        out_specs=pl.BlockSpec((BLOCK, HEAD_DIM), lambda b, h, i: (b * nq + i, h)),
        out_shape=jax.ShapeDtypeStruct((nt, n_heads * HEAD_DIM), BF16),
        compiler_params=_params("arbitrary", "arbitrary", "arbitrary"),
        name="sb_attn",
    )(proj, proj, proj)


def _rope_heads_kernel(x_ref, c_ref, s_ref, o_ref, *, n_heads):
    c, s = c_ref[...], s_ref[...]
    for h in range(n_heads):
        sl = slice(h * HEAD_DIM, (h + 1) * HEAD_DIM)
        o_ref[:, sl] = _rope(x_ref[:, sl].astype(F32), c, s).astype(o_ref.dtype)


def _rope_heads(x, col_blk, n_heads, cos2, sin2, tr=512):
    nt = x.shape[0]
    tr = min(tr, nt)
    w = n_heads * HEAD_DIM
    return pl.pallas_call(
        functools.partial(_rope_heads_kernel, n_heads=n_heads),
        grid=(nt // tr,),
        in_specs=[
            pl.BlockSpec((tr, w), lambda i: (i, col_blk)),
            pl.BlockSpec((tr, HEAD_DIM), lambda i: (i, 0)),
            pl.BlockSpec((tr, HEAD_DIM), lambda i: (i, 0)),
        ],
        out_specs=pl.BlockSpec((tr, w), lambda i: (i, 0)),
        out_shape=jax.ShapeDtypeStruct((nt, w), BF16),
        compiler_params=_params("arbitrary"),
        name="rope_k",
    )(x, cos2, sin2)


def _idx_key_kernel(t_ref, g_ref, c_ref, s_ref, o_ref):
    x = t_ref[:, 0:HEAD_DIM]
    y = x * lax.rsqrt(jnp.mean(x * x, axis=-1, keepdims=True) + NORM_EPS) * g_ref[...]
    o_ref[...] = _rope(y, c_ref[...], s_ref[...]).astype(o_ref.dtype)


def _idx_key(tail, g, cos2, sin2, tr=512):
    nt, tw = tail.shape
    tr = min(tr, nt)
    return pl.pallas_call(
        _idx_key_kernel,
        grid=(nt // tr,),
        in_specs=[
            pl.BlockSpec((tr, tw), lambda i: (i, 0)),
            pl.BlockSpec((1, HEAD_DIM), lambda i: (0, 0)),
            pl.BlockSpec((tr, HEAD_DIM), lambda i: (i, 0)),
            pl.BlockSpec((tr, HEAD_DIM), lambda i: (i, 0)),
        ],
        out_specs=pl.BlockSpec((tr, HEAD_DIM), lambda i: (i, 0)),
        out_shape=jax.ShapeDtypeStruct((nt, HEAD_DIM), BF16),
        compiler_params=_params("arbitrary"),
        name="idx_key",
    )(tail, g.reshape(1, HEAD_DIM), cos2, sin2)


def _idx_kernel(qi_ref, kir_ref, tail_ref, c_ref, s_ref, o_ref, qr_ref, wb_ref, sk_ref,
                *, n_ih, kw, topk, idx_scale, n_chunks):
    i = pl.program_id(1)
    cq, sq = c_ref[...], s_ref[...]
    wi = tail_ref[:, HEAD_DIM:2 * HEAD_DIM]
    for h in range(n_ih):
        hs = slice(h * BLOCK, (h + 1) * BLOCK)
        qr_ref[hs, :] = _rope(qi_ref[:, hs].astype(F32), cq, sq).astype(BF16)
        wb_ref[h] = jnp.broadcast_to(wi[:, h:h + 1], (BLOCK, kw))
    nc = ((i + 1) * BLOCK + kw - 1) // kw
    qidx = i * BLOCK + lax.broadcasted_iota(jnp.int32, (BLOCK, kw), 0)
    kloc = lax.broadcasted_iota(jnp.int32, (BLOCK, kw), 1)

    def score_chunk(c, _):
        off = pl.multiple_of(c * kw, kw)
        kc = kir_ref[pl.ds(off, kw), :]
        lg = _dot_nt(qr_ref[...], kc)
        acc = jnp.zeros((BLOCK, kw), F32)
        for h in range(n_ih):
            acc = acc + jnp.maximum(lg[h * BLOCK:(h + 1) * BLOCK], 0.0) * wb_ref[h]
        score = jnp.where(c * kw + kloc <= qidx, acc * idx_scale, -jnp.inf)
        bits = lax.bitcast_convert_type(score, jnp.int32)
        bits = jnp.where(bits == INT_MIN, 0, bits)
        sk_ref[c] = jnp.where(bits < 0, bits ^ 0x7FFFFFFF, bits)
        return 0

    lax.fori_loop(0, nc, score_chunk, 0)

    def count_ge(cand):
        cb = jnp.broadcast_to(cand, (BLOCK, kw))

        def body(c, acc):
            return acc + (sk_ref[c] >= cb).astype(jnp.int32)

        acc = lax.fori_loop(0, nc, body, jnp.zeros((BLOCK, kw), jnp.int32))
        return jnp.sum(acc, axis=-1, keepdims=True)

    zero = jnp.zeros((BLOCK, 1), jnp.int32)
    thr = jnp.where(count_ge(zero) >= topk, zero, INT_MIN)

    def bit_step(t, thr):
        cand = thr | (jnp.int32(1) << (30 - t))
        return jnp.where(count_ge(cand) >= topk, cand, thr)

    thr = lax.fori_loop(0, 31, bit_step, thr)
    tb = jnp.broadcast_to(thr, (BLOCK, kw))
    for c in range(n_chunks):
        @pl.when(c < nc)
        def _():
            sel = (sk_ref[c] >= tb) & (c * kw + kloc <= qidx)
            o_ref[0, c] = jnp.where(sel, 0.0, NEG).astype(o_ref.dtype)

        @pl.when(c >= nc)
        def _():
            o_ref[0, c] = jnp.full((BLOCK, kw), NEG, o_ref.dtype)


def _indexer(qi, kir, tail, cos2, sin2, batch, seq, n_ih, topk, kw):
    nt = qi.shape[0]
    nq = seq // BLOCK
    n_chunks = seq // kw
    idx_scale = (n_ih ** -0.5) * (HEAD_DIM ** -0.5)
    return pl.pallas_call(
        functools.partial(_idx_kernel, n_ih=n_ih, kw=kw, topk=topk, idx_scale=idx_scale,
                          n_chunks=n_chunks),
        grid=(batch, nq),
        in_specs=[
            pl.BlockSpec((BLOCK, n_ih * HEAD_DIM), lambda b, i: (b * nq + i, 0)),
            pl.BlockSpec((seq, HEAD_DIM), lambda b, i: (b, 0)),
            pl.BlockSpec((BLOCK, 2 * HEAD_DIM), lambda b, i: (b * nq + i, 0)),
            pl.BlockSpec((BLOCK, HEAD_DIM), lambda b, i: (b * nq + i, 0)),
            pl.BlockSpec((BLOCK, HEAD_DIM), lambda b, i: (b * nq + i, 0)),
        ],
        out_specs=pl.BlockSpec((1, n_chunks, BLOCK, kw), lambda b, i: (b * nq + i, 0, 0, 0)),
        out_shape=jax.ShapeDtypeStruct((batch * nq, n_chunks, BLOCK, kw), BF16),
        scratch_shapes=[
            pltpu.VMEM((n_ih * BLOCK, HEAD_DIM), BF16),
            pltpu.VMEM((n_ih, BLOCK, kw), F32),
            pltpu.VMEM((n_chunks, BLOCK, kw), jnp.int32),
        ],
        compiler_params=_params("arbitrary", "arbitrary"),
        name="dsa_indexer",
    )(qi, kir, tail, cos2, sin2)


def _dsa_kernel(q_ref, k_ref, v_ref, bias_ref, c_ref, s_ref, o_ref, *, group, kw, scale):
    i = pl.program_id(2)
    cq, sq = c_ref[...], s_ref[...]
    q = jnp.concatenate(
        [_rope(q_ref[:, g * HEAD_DIM:(g + 1) * HEAD_DIM].astype(F32), cq, sq).astype(BF16)
         for g in range(group)], axis=0)
    nc = ((i + 1) * BLOCK + kw - 1) // kw
    rows = group * BLOCK

    def body(c, carry):
        m, l, acc = carry
        off = pl.multiple_of(c * kw, kw)
        k = k_ref[pl.ds(off, kw), :]
        v = v_ref[pl.ds(off, kw), :]
        b = bias_ref[0, c].astype(F32)
        s = _dot_nt(q, k) * scale + jnp.concatenate([b] * group, axis=0)
        m_new = jnp.maximum(m, jnp.max(s, axis=-1, keepdims=True))
        alpha = jnp.exp(m - m_new)
        p = jnp.exp(s - m_new)
        l = l * alpha + jnp.sum(p, axis=-1, keepdims=True)
        acc = acc * alpha + _dot(p.astype(BF16), v)
        return m_new, l, acc

    m, l, acc = lax.fori_loop(
        0, nc, body,
        (jnp.full((rows, 1), NEG, F32), jnp.zeros((rows, 1), F32), jnp.zeros((rows, HEAD_DIM), F32)))
    o = acc / l
    for g in range(group):
        o_ref[:, g * HEAD_DIM:(g + 1) * HEAD_DIM] = o[g * BLOCK:(g + 1) * BLOCK].astype(o_ref.dtype)


def _dsa(qkv, kr, bias, cos2, sin2, batch, seq, n_heads, n_kv, kw):
    nt = qkv.shape[0]
    nq = seq // BLOCK
    group = n_heads // n_kv
    n_chunks = seq // kw
    vcol = n_heads + n_kv
    return pl.pallas_call(
        functools.partial(_dsa_kernel, group=group, kw=kw, scale=HEAD_DIM ** -0.5),
        grid=(batch, n_kv, nq),
        in_specs=[
            pl.BlockSpec((BLOCK, group * HEAD_DIM), lambda b, h, i: (b * nq + i, h)),
            pl.BlockSpec((seq, HEAD_DIM), lambda b, h, i: (b, h)),
            pl.BlockSpec((seq, HEAD_DIM), lambda b, h, i: (b, vcol + h)),
            pl.BlockSpec((1, n_chunks, BLOCK, kw), lambda b, h, i: (b * nq + i, 0, 0, 0)),
            pl.BlockSpec((BLOCK, HEAD_DIM), lambda b, h, i: (b * nq + i, 0)),
            pl.BlockSpec((BLOCK, HEAD_DIM), lambda b, h, i: (b * nq + i, 0)),
        ],
        out_specs=pl.BlockSpec((BLOCK, group * HEAD_DIM), lambda b, h, i: (b * nq + i, h)),
        out_shape=jax.ShapeDtypeStruct((nt, n_heads * HEAD_DIM), BF16),
        compiler_params=_params("arbitrary", "arbitrary", "arbitrary"),
        name="dsa_attn",
    )(qkv, kr, qkv, bias, cos2, sin2)


def _rope_tables(positions):
    inv_freq = ROPE_THETA ** (-jnp.arange(0, HEAD_DIM, 2, dtype=F32) / HEAD_DIM)
    ang = positions.astype(F32).reshape(-1)[:, None] * inv_freq
    cos, sin = jnp.cos(ang), jnp.sin(ang)
    return jnp.concatenate([cos, cos], axis=-1), jnp.concatenate([-sin, sin], axis=-1)


def _pick_tile(n, prefs):
    for t in prefs:
        if n % t == 0:
            return t
    return n


def kernel(x, c, positions, norm1_g, norm2_g, ada_w, ada_b, even_w_in, even_sinks, even_w_o,
           odd_w_in, odd_idx_k_g, odd_w_o, ffn_w_gate, ffn_w_up, ffn_conv_w, ffn_conv_b,
           ffn_w_down, final_g):
    batch, seq, d = x.shape
    depth = norm1_g.shape[0]
    nt = batch * seq
    a_heads = even_sinks.shape[1]
    a_kv = max(1, a_heads // 8)
    b_heads = even_w_o.shape[1] // HEAD_DIM - a_heads
    c_heads = odd_w_o.shape[1] // HEAD_DIM
    c_kv = c_heads // 4
    odd_in = odd_w_in.shape[2]
    n_ih = (odd_in - (c_heads + 2 * c_kv) * HEAD_DIM - HEAD_DIM) // (HEAD_DIM + 1)
    topk = min(TOPK_MAX, seq // 4)
    kw = min(256, seq)

    cos2, sin2 = _rope_tables(positions)
    c_pad = jnp.zeros((8, d), F32).at[:batch].set(c)
    mods = _ada(c_pad, ada_w, ada_b, tn=_pick_tile(6 * d, (512, 256, 128)))[:, :batch]
    mods = mods.reshape(depth, batch, 6, 1, d)

    tm = 1024
    h = x.reshape(nt, d)
    for layer in range(depth):
        shift1, scale1, gate1, shift2, scale2, gate2 = [mods[layer, :, k] for k in range(6)]
        j = layer // 2
        hn = _norm_mod(h, norm1_g[layer], scale1, shift1, seq)
        if layer % 2 == 0:
            w_in = even_w_in[j].astype(BF16)
            n_in = w_in.shape[1]
            proj = _matmul([hn], w_in, n_in, 0, tm, _pick_tile(n_in, (512, 256, 128)), BF16, "in_proj_even")
            oa = _swa(proj, cos2, sin2, even_sinks[j], batch, seq, a_heads, a_kv)
            qcol = a_heads + 2 * a_kv
            ob = _sb(proj, batch, seq, b_heads, qcol, qcol + b_heads, qcol + 2 * b_heads)
            w_o = even_w_o[j].astype(BF16)
            if a_heads == b_heads:
                parts = [oa, ob]
            else:
                parts = [jnp.concatenate([oa, ob], axis=-1)]
            h = _matmul(parts, w_o, d, 0, tm, _pick_tile(d, (512, 256, 128)), F32, "out_proj_even",
                        res=h, gate=gate1, seq=seq)
        else:
            w_in = odd_w_in[j].astype(BF16)
            n_qkv = (c_heads + 2 * c_kv) * HEAD_DIM
            n_qi = n_ih * HEAD_DIM
            qkv = _matmul([hn], w_in, n_qkv, 0, tm, _pick_tile(n_qkv, (512, 256, 128)), BF16, "in_proj_qkv")
            tn_qi = _pick_tile(n_qi, [t for t in (512, 256, 128) if n_qkv % t == 0])
            qi = _matmul([hn], w_in, n_qi, n_qkv, tm, tn_qi, BF16, "in_proj_qi")
            w_tail = jnp.zeros((d, 2 * HEAD_DIM), BF16).at[:, :HEAD_DIM + n_ih].set(w_in[:, n_qkv + n_qi:])
            tail = _matmul([hn], w_tail, 2 * HEAD_DIM, 0, tm, 2 * HEAD_DIM, F32, "in_proj_idx")
            kr = _rope_heads(qkv, c_heads // c_kv, c_kv, cos2, sin2)
            kir = _idx_key(tail, odd_idx_k_g[j], cos2, sin2)
            bias = _indexer(qi, kir, tail, cos2, sin2, batch, seq, n_ih, topk, kw)
            o = _dsa(qkv, kr, bias, cos2, sin2, batch, seq, c_heads, c_kv, kw)
            w_o = odd_w_o[j].astype(BF16)
            h = _matmul([o], w_o, d, 0, tm, _pick_tile(d, (512, 256, 128)), F32, "out_proj_odd",
                        res=h, gate=gate1, seq=seq)
        hn = _norm_mod(h, norm2_g[layer], scale2, shift2, seq)
        a = _ffn_up(hn, ffn_w_gate[layer].astype(BF16), ffn_w_up[layer].astype(BF16),
                    ffn_conv_w[layer], ffn_conv_b[layer], seq)
        h = _matmul([a], ffn_w_down[layer].astype(BF16), d, 0, 512, 256, F32, "ffn_down",
                    res=h, gate=gate2, seq=seq)
    return _norm_plain(h, final_g).reshape(batch, seq, d)
```

```python
import functools

import jax
import jax.numpy as jnp
from jax import lax
from jax.experimental import pallas as pl
from jax.experimental.pallas import tpu as pltpu

F32 = jnp.float32
BF16 = jnp.bfloat16

HEAD_DIM = 128
BLOCK = 128
ROPE_THETA = 10000.0
NORM_EPS = 1e-6
TOPK_MAX = 256
NEG = -1e30
INT_MIN = -(2 ** 31)
VMEM_LIMIT_BYTES = 56 * 1024 * 1024


def _params(*sem):
    return pltpu.CompilerParams(dimension_semantics=sem, vmem_limit_bytes=VMEM_LIMIT_BYTES)


def _dot(a, b):
    return jnp.dot(a, b, preferred_element_type=F32)


def _dot_nt(a, b):
    return lax.dot_general(a, b, (((1,), (1,)), ((), ())), preferred_element_type=F32)


def _rope(x, cos2, sin2):
    return x * cos2 + pltpu.roll(x, HEAD_DIM // 2, axis=1) * sin2


def _ada_kernel(c_ref, w_ref, b_ref, o_ref):
    c = c_ref[...]
    ca = (c * jax.nn.sigmoid(c)).astype(BF16)
    o_ref[0] = _dot(ca, w_ref[0].astype(BF16)) + b_ref[0]


def _ada(c_pad, ada_w, ada_b, tn=512):
    n_layers, d, n = ada_w.shape
    rows = c_pad.shape[0]
    return pl.pallas_call(
        _ada_kernel,
        grid=(n_layers, n // tn),
        in_specs=[
            pl.BlockSpec((rows, d), lambda l, j: (0, 0)),
            pl.BlockSpec((1, d, tn), lambda l, j: (l, 0, j)),
            pl.BlockSpec((1, 1, tn), lambda l, j: (l, 0, j)),
        ],
        out_specs=pl.BlockSpec((1, rows, tn), lambda l, j: (l, 0, j)),
        out_shape=jax.ShapeDtypeStruct((n_layers, rows, n), F32),
        compiler_params=_params("arbitrary", "arbitrary"),
        name="ada_mod",
    )(c_pad, ada_w, ada_b.reshape(n_layers, 1, n))


def _norm_mod_kernel(x_ref, g_ref, sc_ref, sh_ref, o_ref):
    x = x_ref[...]
    y = x * lax.rsqrt(jnp.mean(x * x, axis=-1, keepdims=True) + NORM_EPS) * g_ref[...]
    o_ref[...] = (y * (1.0 + sc_ref[0]) + sh_ref[0]).astype(o_ref.dtype)


def _norm_plain_kernel(x_ref, g_ref, o_ref):
    x = x_ref[...]
    y = x * lax.rsqrt(jnp.mean(x * x, axis=-1, keepdims=True) + NORM_EPS) * g_ref[...]
    o_ref[...] = y.astype(o_ref.dtype)


def _norm_mod(h, g, scale, shift, seq, tr=256):
    nt, d = h.shape
    tr = min(tr, seq)
    bps = seq // tr
    return pl.pallas_call(
        _norm_mod_kernel,
        grid=(nt // tr,),
        in_specs=[
            pl.BlockSpec((tr, d), lambda i: (i, 0)),
            pl.BlockSpec((1, d), lambda i: (0, 0)),
            pl.BlockSpec((1, 1, d), lambda i: (i // bps, 0, 0)),
            pl.BlockSpec((1, 1, d), lambda i: (i // bps, 0, 0)),
        ],
        out_specs=pl.BlockSpec((tr, d), lambda i: (i, 0)),
        out_shape=jax.ShapeDtypeStruct((nt, d), BF16),
        compiler_params=_params("arbitrary"),
        name="norm_mod",
    )(h, g.reshape(1, d), scale, shift)


def _norm_plain(h, g, tr=256):
    nt, d = h.shape
    tr = min(tr, nt)
    return pl.pallas_call(
        _norm_plain_kernel,
        grid=(nt // tr,),
        in_specs=[pl.BlockSpec((tr, d), lambda i: (i, 0)), pl.BlockSpec((1, d), lambda i: (0, 0))],
        out_specs=pl.BlockSpec((tr, d), lambda i: (i, 0)),
        out_shape=jax.ShapeDtypeStruct((nt, d), F32),
        compiler_params=_params("arbitrary"),
        name="norm_final",
    )(h, g.reshape(1, d))


def _mm_kernel(*refs, n_parts, has_res):
    acc = None
    for x_ref, w_ref in zip(refs[:n_parts], refs[n_parts:2 * n_parts]):
        d = _dot(x_ref[...], w_ref[...])
        acc = d if acc is None else acc + d
    rest = refs[2 * n_parts:]
    if has_res:
        res_ref, gate_ref, o_ref = rest
        o_ref[...] = res_ref[...] + gate_ref[0] * acc
    else:
        (o_ref,) = rest
        o_ref[...] = acc.astype(o_ref.dtype)


def _matmul(xs, w, n_out, col_off, tm, tn, out_dtype, name, res=None, gate=None, seq=None):
    m, kp = xs[0].shape
    tm = min(tm, m if seq is None else seq)
    tn = min(tn, n_out)
    assert m % tm == 0 and n_out % tn == 0 and col_off % tn == 0
    cb = col_off // tn
    n_parts = len(xs)
    in_specs = [pl.BlockSpec((tm, kp), lambda i, j: (i, 0)) for _ in xs]
    in_specs += [pl.BlockSpec((kp, tn), lambda i, j, p=p: (p, j + cb)) for p in range(n_parts)]
    args = list(xs) + [w] * n_parts
    if res is not None:
        bps = seq // tm
        in_specs += [pl.BlockSpec((tm, tn), lambda i, j: (i, j)),
                     pl.BlockSpec((1, 1, tn), lambda i, j: (i // bps, 0, j))]
        args += [res, gate]
    return pl.pallas_call(
        functools.partial(_mm_kernel, n_parts=n_parts, has_res=res is not None),
        grid=(m // tm, n_out // tn),
        in_specs=in_specs,
        out_specs=pl.BlockSpec((tm, tn), lambda i, j: (i, j)),
        out_shape=jax.ShapeDtypeStruct((m, n_out), out_dtype),
        compiler_params=_params("arbitrary", "arbitrary"),
        name=name,
    )(*args)


HALO = 16


def _ffn_up_kernel(x_ref, xh_ref, wg_ref, wu_ref, cw_ref, cb_ref, o_ref, gs_ref, *, tm, bps):
    i = pl.program_id(0)
    x = x_ref[...]
    wg = wg_ref[...]
    g = _dot(x, wg)
    u = _dot(x, wu_ref[...])
    gh = _dot(xh_ref[...], wg)
    gs_ref[0:HALO, :] = jnp.where(i % bps == 0, 0.0, gh)
    gs_ref[HALO:, :] = g
    g1 = gs_ref[HALO - 1:HALO - 1 + tm, :]
    g2 = gs_ref[HALO - 2:HALO - 2 + tm, :]
    cw = cw_ref[...]
    gc = cw[0:1] * g2 + cw[1:2] * g1 + cw[2:3] * g + cb_ref[...]
    o_ref[...] = (gc * jax.nn.sigmoid(gc) * u).astype(o_ref.dtype)


def _ffn_up(hn, wg, wu, conv_w, conv_b, seq, tm=1024, tn=256):
    m, d = hn.shape
    dff = wg.shape[1]
    tm = min(tm, seq)
    assert dff % tn == 0 and tm % HALO == 0
    bps = seq // tm
    hpb = tm // HALO
    return pl.pallas_call(
        functools.partial(_ffn_up_kernel, tm=tm, bps=bps),
        grid=(m // tm, dff // tn),
        in_specs=[
            pl.BlockSpec((tm, d), lambda i, j: (i, 0)),
            pl.BlockSpec((HALO, d), lambda i, j: (jnp.maximum(i * hpb - 1, 0), 0)),
            pl.BlockSpec((d, tn), lambda i, j: (0, j)),
            pl.BlockSpec((d, tn), lambda i, j: (0, j)),
            pl.BlockSpec((3, tn), lambda i, j: (0, j)),
            pl.BlockSpec((1, tn), lambda i, j: (0, j)),
        ],
        out_specs=pl.BlockSpec((tm, tn), lambda i, j: (i, j)),
        out_shape=jax.ShapeDtypeStruct((m, dff), BF16),
        scratch_shapes=[pltpu.VMEM((tm + HALO, tn), F32)],
        compiler_params=_params("arbitrary", "arbitrary"),
        name="ffn_up",
    )(hn, hn, wg, wu, conv_w, conv_b.reshape(1, dff))


def _swa_kernel(sink_ref, q_ref, kc_ref, kp_ref, vc_ref, vp_ref, cq_ref, sq_ref, cp_ref, sp_ref,
                o_ref, *, n_heads, n_kv, scale):
    i = pl.program_id(1)
    cq, sq, cp, sp = cq_ref[...], sq_ref[...], cp_ref[...], sp_ref[...]
    group = n_heads // n_kv
    qi = lax.broadcasted_iota(jnp.int32, (BLOCK, BLOCK), 0)
    kj = lax.broadcasted_iota(jnp.int32, (BLOCK, BLOCK), 1)
    mask_c = kj <= qi
    mask_p = (kj > qi) & (i > 0)
    for hk in range(n_kv):
        sl = slice(hk * HEAD_DIM, (hk + 1) * HEAD_DIM)
        kc = _rope(kc_ref[:, sl].astype(F32), cq, sq).astype(BF16)
        kp = _rope(kp_ref[:, sl].astype(F32), cp, sp).astype(BF16)
        vc = vc_ref[:, sl]
        vp = vp_ref[:, sl]
        for g in range(group):
            h = hk * group + g
            hs = slice(h * HEAD_DIM, (h + 1) * HEAD_DIM)
            q = _rope(q_ref[:, hs].astype(F32), cq, sq).astype(BF16)
            s_c = jnp.where(mask_c, _dot_nt(q, kc) * scale, NEG)
            s_p = jnp.where(mask_p, _dot_nt(q, kp) * scale, NEG)
            sink = sink_ref[h]
            m = jnp.maximum(jnp.max(s_c, axis=-1, keepdims=True), jnp.max(s_p, axis=-1, keepdims=True))
            m = jnp.maximum(m, sink)
            p_c = jnp.exp(s_c - m)
            p_p = jnp.exp(s_p - m)
            den = (jnp.sum(p_c, axis=-1, keepdims=True) + jnp.sum(p_p, axis=-1, keepdims=True)
                   + jnp.exp(sink - m))
            o = _dot(p_c.astype(BF16), vc) + _dot(p_p.astype(BF16), vp)
            o_ref[:, hs] = (o / den).astype(o_ref.dtype)


def _swa(proj, cos2, sin2, sinks, batch, seq, n_heads, n_kv):
    nt = proj.shape[0]
    nq = seq // BLOCK
    qw, kw = n_heads * HEAD_DIM, n_kv * HEAD_DIM
    kcol = n_heads // n_kv
    vcol = kcol + 1
    cur = lambda b, i: b * nq + i
    prev = lambda b, i: b * nq + jnp.maximum(i - 1, 0)
    return pl.pallas_call(
        functools.partial(_swa_kernel, n_heads=n_heads, n_kv=n_kv, scale=HEAD_DIM ** -0.5),
        grid=(batch, nq),
        in_specs=[
            pl.BlockSpec(memory_space=pltpu.SMEM),
            pl.BlockSpec((BLOCK, qw), lambda b, i: (cur(b, i), 0)),
            pl.BlockSpec((BLOCK, kw), lambda b, i: (cur(b, i), kcol)),
            pl.BlockSpec((BLOCK, kw), lambda b, i: (prev(b, i), kcol)),
            pl.BlockSpec((BLOCK, kw), lambda b, i: (cur(b, i), vcol)),
            pl.BlockSpec((BLOCK, kw), lambda b, i: (prev(b, i), vcol)),
            pl.BlockSpec((BLOCK, HEAD_DIM), lambda b, i: (cur(b, i), 0)),
            pl.BlockSpec((BLOCK, HEAD_DIM), lambda b, i: (cur(b, i), 0)),
            pl.BlockSpec((BLOCK, HEAD_DIM), lambda b, i: (prev(b, i), 0)),
            pl.BlockSpec((BLOCK, HEAD_DIM), lambda b, i: (prev(b, i), 0)),
        ],
        out_specs=pl.BlockSpec((BLOCK, qw), lambda b, i: (cur(b, i), 0)),
        out_shape=jax.ShapeDtypeStruct((nt, qw), BF16),
        compiler_params=_params("arbitrary", "arbitrary"),
        name="swa_attn",
    )(sinks, proj, proj, proj, proj, proj, cos2, sin2, cos2, sin2)


SB_TILE = 256
SB_DROP_CAP = 104.0


def _sb_kernel(q_ref, k_ref, v_ref, o_ref, acc_ref, run_ref, *, scale, tile):
    i = pl.program_id(2)
    q = q_ref[...]
    row = lax.broadcasted_iota(jnp.int32, (tile, tile), 0)
    col = lax.broadcasted_iota(jnp.int32, (tile, tile), 1)
    strict = col < row
    rr = lax.broadcasted_iota(jnp.int32, (tile, tile + HEAD_DIM), 0)
    cc = lax.broadcasted_iota(jnp.int32, (tile, tile + HEAD_DIM), 1)
    cum = ((rr > cc) | (cc >= tile)).astype(BF16)

    def step(j, diag):
        off = pl.multiple_of(j * tile, tile)
        k = k_ref[pl.ds(off, tile), :]
        v = v_ref[pl.ds(off, tile), :]
        z = _dot_nt(q, k) * scale
        soft = jnp.log(1.0 + jnp.exp(-jnp.abs(z)))
        log_beta = jnp.minimum(z, 0.0) - soft
        drop = jnp.maximum(z, 0.0) + soft
        if diag:
            drop = jnp.where(strict, drop, 0.0)
        hi = drop.astype(BF16)
        lo = (drop - hi.astype(F32)).astype(BF16)
        sums = _dot(hi, cum) + _dot(lo, cum)
        after = sums[:, :tile]
        if not diag:
            run = run_ref[...]
            after = after + jnp.concatenate([run] * (tile // HEAD_DIM), axis=1)
        w = jnp.exp(log_beta - after)
        if diag:
            w = jnp.where(strict, w, 0.0)
        pv = _dot(w.astype(BF16), v)
        if diag:
            acc_ref[...] = pv
            run = sums[:, tile:]
        else:
            acc_ref[...] += pv
            run = run + sums[:, tile:]
        run_ref[...] = run
        return jnp.min(run)

    def cond(st):
        return (st[0] >= 0) & (st[1] < SB_DROP_CAP)

    def body(st):
        return st[0] - 1, step(st[0], False)

    lax.while_loop(cond, body, (i - 1, step(i, True)))
    o_ref[...] = acc_ref[...].astype(o_ref.dtype)


def _sb(proj, batch, seq, n_heads, qcol, kcol, vcol):
    nt = proj.shape[0]
    tile = min(SB_TILE, seq)
    nq = seq // tile
    return pl.pallas_call(
        functools.partial(_sb_kernel, scale=HEAD_DIM ** -0.5, tile=tile),
        grid=(batch, n_heads, nq),
        in_specs=[
            pl.BlockSpec((tile, HEAD_DIM), lambda b, h, i: (b * nq + i, qcol + h)),
            pl.BlockSpec((seq, HEAD_DIM), lambda b, h, i: (b, kcol + h)),
            pl.BlockSpec((seq, HEAD_DIM), lambda b, h, i: (b, vcol + h)),
        ],
        out_specs=pl.BlockSpec((tile, HEAD_DIM), lambda b, h, i: (b * nq + i, h)),
        out_shape=jax.ShapeDtypeStruct((nt, n_heads * HEAD_DIM), BF16),
        scratch_shapes=[pltpu.VMEM((tile, HEAD_DIM), F32), pltpu.VMEM((tile, HEAD_DIM), F32)],
        compiler_params=_params("arbitrary", "arbitrary", "arbitrary"),
        name="sb_attn",
    )(proj, proj, proj)


def _rope_heads_kernel(x_ref, c_ref, s_ref, o_ref, *, n_heads):
    c, s = c_ref[...], s_ref[...]
    for h in range(n_heads):
        sl = slice(h * HEAD_DIM, (h + 1) * HEAD_DIM)
        o_ref[:, sl] = _rope(x_ref[:, sl].astype(F32), c, s).astype(o_ref.dtype)


def _rope_heads(x, col_blk, n_heads, cos2, sin2, tr=512):
    nt = x.shape[0]
    tr = min(tr, nt)
    w = n_heads * HEAD_DIM
    return pl.pallas_call(
        functools.partial(_rope_heads_kernel, n_heads=n_heads),
        grid=(nt // tr,),
        in_specs=[
            pl.BlockSpec((tr, w), lambda i: (i, col_blk)),
            pl.BlockSpec((tr, HEAD_DIM), lambda i: (i, 0)),
            pl.BlockSpec((tr, HEAD_DIM), lambda i: (i, 0)),
        ],
        out_specs=pl.BlockSpec((tr, w), lambda i: (i, 0)),
        out_shape=jax.ShapeDtypeStruct((nt, w), BF16),
        compiler_params=_params("arbitrary"),
        name="rope_k",
    )(x, cos2, sin2)


def _idx_key_kernel(t_ref, g_ref, c_ref, s_ref, o_ref):
    x = t_ref[:, 0:HEAD_DIM]
    y = x * lax.rsqrt(jnp.mean(x * x, axis=-1, keepdims=True) + NORM_EPS) * g_ref[...]
    o_ref[...] = _rope(y, c_ref[...], s_ref[...]).astype(o_ref.dtype)


def _idx_key(tail, g, cos2, sin2, tr=512):
    nt, tw = tail.shape
    tr = min(tr, nt)
    return pl.pallas_call(
        _idx_key_kernel,
        grid=(nt // tr,),
        in_specs=[
            pl.BlockSpec((tr, tw), lambda i: (i, 0)),
            pl.BlockSpec((1, HEAD_DIM), lambda i: (0, 0)),
            pl.BlockSpec((tr, HEAD_DIM), lambda i: (i, 0)),
            pl.BlockSpec((tr, HEAD_DIM), lambda i: (i, 0)),
        ],
        out_specs=pl.BlockSpec((tr, HEAD_DIM), lambda i: (i, 0)),
        out_shape=jax.ShapeDtypeStruct((nt, HEAD_DIM), BF16),
        compiler_params=_params("arbitrary"),
        name="idx_key",
    )(tail, g.reshape(1, HEAD_DIM), cos2, sin2)


def _idx_kernel(qi_ref, kir_ref, tail_ref, c_ref, s_ref, o_ref, qr_ref, wb_ref, sk_ref,
                *, n_ih, kw, topk, idx_scale, n_chunks):
    i = pl.program_id(1)
    cq, sq = c_ref[...], s_ref[...]
    wi = tail_ref[:, HEAD_DIM:2 * HEAD_DIM]
    for h in range(n_ih):
        hs = slice(h * BLOCK, (h + 1) * BLOCK)
        qr_ref[hs, :] = _rope(qi_ref[:, hs].astype(F32), cq, sq).astype(BF16)
        wb_ref[h] = jnp.broadcast_to(wi[:, h:h + 1], (BLOCK, HEAD_DIM))
    nc = ((i + 1) * BLOCK + kw - 1) // kw
    n_lt = kw // HEAD_DIM
    qidx = i * BLOCK + lax.broadcasted_iota(jnp.int32, (BLOCK, kw), 0)
    kloc = lax.broadcasted_iota(jnp.int32, (BLOCK, kw), 1)

    def score_chunk(c, _):
        off = pl.multiple_of(c * kw, kw)
        kc = kir_ref[pl.ds(off, kw), :]
        lg = _dot_nt(qr_ref[...], kc)
        accs = [jnp.zeros((BLOCK, HEAD_DIM), F32) for _ in range(n_lt)]
        for h in range(n_ih):
            w = wb_ref[h]
            for t in range(n_lt):
                tile = lg[h * BLOCK:(h + 1) * BLOCK, t * HEAD_DIM:(t + 1) * HEAD_DIM]
                accs[t] = accs[t] + jnp.maximum(tile, 0.0) * w
        acc = jnp.concatenate(accs, axis=1)
        score = jnp.where(c * kw + kloc <= qidx, acc * idx_scale, -jnp.inf)
        bits = lax.bitcast_convert_type(score, jnp.int32)
        bits = jnp.where(bits == INT_MIN, 0, bits)
        sk_ref[c] = jnp.where(bits < 0, bits ^ 0x7FFFFFFF, bits)
        return 0

    lax.fori_loop(0, nc, score_chunk, 0)

    def count_ge(cand):
        cb = jnp.broadcast_to(cand, (BLOCK, HEAD_DIM))

        def body(c, acc):
            x = sk_ref[c]
            for t in range(n_lt):
                acc = acc + (x[:, t * HEAD_DIM:(t + 1) * HEAD_DIM] >= cb).astype(jnp.int32)
            return acc

        acc = lax.fori_loop(0, nc, body, jnp.zeros((BLOCK, HEAD_DIM), jnp.int32))
        return jnp.sum(acc, axis=-1, keepdims=True)

    zero = jnp.zeros((BLOCK, 1), jnp.int32)
    thr = jnp.where(count_ge(zero) >= topk, zero, INT_MIN)

    def bit_step(t, thr):
        cand = thr | (jnp.int32(1) << (30 - t))
        return jnp.where(count_ge(cand) >= topk, cand, thr)

    thr = lax.fori_loop(0, 31, bit_step, thr)
    tb = jnp.broadcast_to(thr, (BLOCK, kw))
    for c in range(n_chunks):
        @pl.when(c < nc)
        def _():
            sel = (sk_ref[c] >= tb) & (c * kw + kloc <= qidx)
            o_ref[0, c] = jnp.where(sel, 0.0, NEG).astype(o_ref.dtype)

        @pl.when(c >= nc)
        def _():
            o_ref[0, c] = jnp.full((BLOCK, kw), NEG, o_ref.dtype)


def _indexer(qi, kir, tail, cos2, sin2, batch, seq, n_ih, topk, kw):
    nt = qi.shape[0]
    nq = seq // BLOCK
    n_chunks = seq // kw
    idx_scale = (n_ih ** -0.5) * (HEAD_DIM ** -0.5)
    return pl.pallas_call(
        functools.partial(_idx_kernel, n_ih=n_ih, kw=kw, topk=topk, idx_scale=idx_scale,
                          n_chunks=n_chunks),
        grid=(batch, nq),
        in_specs=[
            pl.BlockSpec((BLOCK, n_ih * HEAD_DIM), lambda b, i: (b * nq + i, 0)),
            pl.BlockSpec((seq, HEAD_DIM), lambda b, i: (b, 0)),
            pl.BlockSpec((BLOCK, 2 * HEAD_DIM), lambda b, i: (b * nq + i, 0)),
            pl.BlockSpec((BLOCK, HEAD_DIM), lambda b, i: (b * nq + i, 0)),
            pl.BlockSpec((BLOCK, HEAD_DIM), lambda b, i: (b * nq + i, 0)),
        ],
        out_specs=pl.BlockSpec((1, n_chunks, BLOCK, kw), lambda b, i: (b * nq + i, 0, 0, 0)),
        out_shape=jax.ShapeDtypeStruct((batch * nq, n_chunks, BLOCK, kw), BF16),
        scratch_shapes=[
            pltpu.VMEM((n_ih * BLOCK, HEAD_DIM), BF16),
            pltpu.VMEM((n_ih, BLOCK, HEAD_DIM), F32),
            pltpu.VMEM((n_chunks, BLOCK, kw), jnp.int32),
        ],
        compiler_params=_params("arbitrary", "arbitrary"),
        name="dsa_indexer",
    )(qi, kir, tail, cos2, sin2)


LOG2E = 1.4426950408889634


def _dsa_kernel(q_ref, k_ref, v_ref, bias_ref, c_ref, s_ref, o_ref, qs_ref, m_ref, l_ref, acc_ref,
                *, group, kw, scale):
    i = pl.program_id(2)
    cq, sq = c_ref[...], s_ref[...]
    for g in range(group):
        qg = _rope(q_ref[:, g * HEAD_DIM:(g + 1) * HEAD_DIM].astype(F32), cq, sq)
        qs_ref[g * BLOCK:(g + 1) * BLOCK, :] = (qg * (scale * LOG2E)).astype(BF16)
        m_ref[g] = jnp.full((BLOCK, HEAD_DIM), NEG, F32)
        l_ref[g] = jnp.zeros((BLOCK, HEAD_DIM), F32)
        acc_ref[g] = jnp.zeros((BLOCK, HEAD_DIM), F32)
    nc = ((i + 1) * BLOCK + kw - 1) // kw
    ones = jnp.ones((kw, HEAD_DIM), BF16)

    def body(c, _):
        off = pl.multiple_of(c * kw, kw)
        k = k_ref[pl.ds(off, kw), :]
        vx = jnp.concatenate([v_ref[pl.ds(off, kw), :], ones], axis=1)
        b = bias_ref[0, c].astype(F32)
        s_all = _dot_nt(qs_ref[...], k)
        ps, alphas = [], []
        for g in range(group):
            s = s_all[g * BLOCK:(g + 1) * BLOCK] + b
            m_old = m_ref[g]
            m_new = jnp.maximum(m_old, jnp.max(s, axis=-1, keepdims=True))
            alphas.append(jnp.exp2(m_old - m_new))
            ps.append(jnp.exp2(s - jnp.concatenate([m_new] * (kw // HEAD_DIM), axis=1)).astype(BF16))
            m_ref[g] = m_new
        pv = _dot(jnp.concatenate(ps, axis=0), vx)
        for g in range(group):
            pg = pv[g * BLOCK:(g + 1) * BLOCK]
            acc_ref[g] = acc_ref[g] * alphas[g] + pg[:, :HEAD_DIM]
            l_ref[g] = l_ref[g] * alphas[g] + pg[:, HEAD_DIM:]
        return 0

    lax.fori_loop(0, nc, body, 0)
    for g in range(group):
        o_ref[:, g * HEAD_DIM:(g + 1) * HEAD_DIM] = (acc_ref[g] / l_ref[g]).astype(o_ref.dtype)


def _dsa(qkv, kr, bias, cos2, sin2, batch, seq, n_heads, n_kv, kw):
    nt = qkv.shape[0]
    nq = seq // BLOCK
    group = n_heads // n_kv
    n_chunks = seq // kw
    vcol = n_heads + n_kv
    return pl.pallas_call(
        functools.partial(_dsa_kernel, group=group, kw=kw, scale=HEAD_DIM ** -0.5),
        grid=(batch, n_kv, nq),
        in_specs=[
            pl.BlockSpec((BLOCK, group * HEAD_DIM), lambda b, h, i: (b * nq + i, h)),
            pl.BlockSpec((seq, HEAD_DIM), lambda b, h, i: (b, h)),
            pl.BlockSpec((seq, HEAD_DIM), lambda b, h, i: (b, vcol + h)),
            pl.BlockSpec((1, n_chunks, BLOCK, kw), lambda b, h, i: (b * nq + i, 0, 0, 0)),
            pl.BlockSpec((BLOCK, HEAD_DIM), lambda b, h, i: (b * nq + i, 0)),
            pl.BlockSpec((BLOCK, HEAD_DIM), lambda b, h, i: (b * nq + i, 0)),
        ],
        out_specs=pl.BlockSpec((BLOCK, group * HEAD_DIM), lambda b, h, i: (b * nq + i, h)),
        out_shape=jax.ShapeDtypeStruct((nt, n_heads * HEAD_DIM), BF16),
        scratch_shapes=[
            pltpu.VMEM((group * BLOCK, HEAD_DIM), BF16),
            pltpu.VMEM((group, BLOCK, HEAD_DIM), F32),
            pltpu.VMEM((group, BLOCK, HEAD_DIM), F32),
            pltpu.VMEM((group, BLOCK, HEAD_DIM), F32),
        ],
        compiler_params=_params("arbitrary", "arbitrary", "arbitrary"),
        name="dsa_attn",
    )(qkv, kr, qkv, bias, cos2, sin2)


def _rope_tables(positions):
    inv_freq = ROPE_THETA ** (-jnp.arange(0, HEAD_DIM, 2, dtype=F32) / HEAD_DIM)
    ang = positions.astype(F32).reshape(-1)[:, None] * inv_freq
    cos, sin = jnp.cos(ang), jnp.sin(ang)
    return jnp.concatenate([cos, cos], axis=-1), jnp.concatenate([-sin, sin], axis=-1)


def _pick_tile(n, prefs):
    for t in prefs:
        if n % t == 0:
            return t
    return n


def kernel(x, c, positions, norm1_g, norm2_g, ada_w, ada_b, even_w_in, even_sinks, even_w_o,
           odd_w_in, odd_idx_k_g, odd_w_o, ffn_w_gate, ffn_w_up, ffn_conv_w, ffn_conv_b,
           ffn_w_down, final_g):
    batch, seq, d = x.shape
    depth = norm1_g.shape[0]
    nt = batch * seq
    a_heads = even_sinks.shape[1]
    a_kv = max(1, a_heads // 8)
    b_heads = even_w_o.shape[1] // HEAD_DIM - a_heads
    c_heads = odd_w_o.shape[1] // HEAD_DIM
    c_kv = c_heads // 4
    odd_in = odd_w_in.shape[2]
    n_ih = (odd_in - (c_heads + 2 * c_kv) * HEAD_DIM - HEAD_DIM) // (HEAD_DIM + 1)
    topk = min(TOPK_MAX, seq // 4)
    kw = min(512, seq)

    cos2, sin2 = _rope_tables(positions)
    c_pad = jnp.zeros((8, d), F32).at[:batch].set(c)
    mods = _ada(c_pad, ada_w, ada_b, tn=_pick_tile(6 * d, (512, 256, 128)))[:, :batch]
    mods = mods.reshape(depth, batch, 6, 1, d)

    tm = 1024
    h = x.reshape(nt, d)
    for layer in range(depth):
        shift1, scale1, gate1, shift2, scale2, gate2 = [mods[layer, :, k] for k in range(6)]
        j = layer // 2
        hn = _norm_mod(h, norm1_g[layer], scale1, shift1, seq)
        if layer % 2 == 0:
            w_in = even_w_in[j].astype(BF16)
            n_in = w_in.shape[1]
            proj = _matmul([hn], w_in, n_in, 0, tm, _pick_tile(n_in, (512, 256, 128)), BF16, "in_proj_even")
            oa = _swa(proj, cos2, sin2, even_sinks[j], batch, seq, a_heads, a_kv)
            qcol = a_heads + 2 * a_kv
            ob = _sb(proj, batch, seq, b_heads, qcol, qcol + b_heads, qcol + 2 * b_heads)
            w_o = even_w_o[j].astype(BF16)
            if a_heads == b_heads:
                parts = [oa, ob]
            else:
                parts = [jnp.concatenate([oa, ob], axis=-1)]
            h = _matmul(parts, w_o, d, 0, tm, _pick_tile(d, (512, 256, 128)), F32, "out_proj_even",
                        res=h, gate=gate1, seq=seq)
        else:
            w_in = odd_w_in[j].astype(BF16)
            n_qkv = (c_heads + 2 * c_kv) * HEAD_DIM
            n_qi = n_ih * HEAD_DIM
            qkv = _matmul([hn], w_in, n_qkv, 0, tm, _pick_tile(n_qkv, (512, 256, 128)), BF16, "in_proj_qkv")
            tn_qi = _pick_tile(n_qi, [t for t in (512, 256, 128) if n_qkv % t == 0])
            qi = _matmul([hn], w_in, n_qi, n_qkv, tm, tn_qi, BF16, "in_proj_qi")
            w_tail = jnp.zeros((d, 2 * HEAD_DIM), BF16).at[:, :HEAD_DIM + n_ih].set(w_in[:, n_qkv + n_qi:])
            tail = _matmul([hn], w_tail, 2 * HEAD_DIM, 0, tm, 2 * HEAD_DIM, F32, "in_proj_idx")
            kr = _rope_heads(qkv, c_heads // c_kv, c_kv, cos2, sin2)
            kir = _idx_key(tail, odd_idx_k_g[j], cos2, sin2)
            bias = _indexer(qi, kir, tail, cos2, sin2, batch, seq, n_ih, topk, kw)
            o = _dsa(qkv, kr, bias, cos2, sin2, batch, seq, c_heads, c_kv, kw)
            w_o = odd_w_o[j].astype(BF16)
            h = _matmul([o], w_o, d, 0, tm, _pick_tile(d, (512, 256, 128)), F32, "out_proj_odd",
                        res=h, gate=gate1, seq=seq)
        hn = _norm_mod(h, norm2_g[layer], scale2, shift2, seq)
        a = _ffn_up(hn, ffn_w_gate[layer].astype(BF16), ffn_w_up[layer].astype(BF16),
                    ffn_conv_w[layer], ffn_conv_b[layer], seq)
        h = _matmul([a], ffn_w_down[layer].astype(BF16), d, 0, 512, 256, F32, "ffn_down",
                    res=h, gate=gate2, seq=seq)
    return _norm_plain(h, final_g).reshape(batch, seq, d)
```

```python
import functools
import math

import jax
import jax.numpy as jnp
from jax import lax
from jax.experimental import pallas as pl
from jax.experimental.pallas import tpu as pltpu

F32 = jnp.float32
BF16 = jnp.bfloat16

HEAD_DIM = 128
BLOCK = 128
ROPE_THETA = 10000.0
NORM_EPS = 1e-6
TOPK_MAX = 256
NEG = -1e30
INT_MIN = -(2 ** 31)
VMEM_LIMIT_BYTES = 56 * 1024 * 1024


def _params(*sem):
    return pltpu.CompilerParams(dimension_semantics=sem, vmem_limit_bytes=VMEM_LIMIT_BYTES)


def _dot(a, b):
    return jnp.dot(a, b, preferred_element_type=F32)


def _dot_nt(a, b):
    return lax.dot_general(a, b, (((1,), (1,)), ((), ())), preferred_element_type=F32)


def _rope(x, cos2, sin2):
    return x * cos2 + pltpu.roll(x, HEAD_DIM // 2, axis=1) * sin2


def _ada_kernel(c_ref, w_ref, b_ref, o_ref):
    c = c_ref[...]
    ca = (c * jax.nn.sigmoid(c)).astype(BF16)
    o_ref[0] = _dot(ca, w_ref[0].astype(BF16)) + b_ref[0]


def _ada(c_pad, ada_w, ada_b, tn=512):
    n_layers, d, n = ada_w.shape
    rows = c_pad.shape[0]
    return pl.pallas_call(
        _ada_kernel,
        grid=(n_layers, n // tn),
        in_specs=[
            pl.BlockSpec((rows, d), lambda l, j: (0, 0)),
            pl.BlockSpec((1, d, tn), lambda l, j: (l, 0, j)),
            pl.BlockSpec((1, 1, tn), lambda l, j: (l, 0, j)),
        ],
        out_specs=pl.BlockSpec((1, rows, tn), lambda l, j: (l, 0, j)),
        out_shape=jax.ShapeDtypeStruct((n_layers, rows, n), F32),
        compiler_params=_params("arbitrary", "arbitrary"),
        name="ada_mod",
    )(c_pad, ada_w, ada_b.reshape(n_layers, 1, n))


def _norm_mod_kernel(x_ref, g_ref, sc_ref, sh_ref, o_ref):
    x = x_ref[...]
    y = x * lax.rsqrt(jnp.mean(x * x, axis=-1, keepdims=True) + NORM_EPS) * g_ref[...]
    o_ref[...] = (y * (1.0 + sc_ref[0]) + sh_ref[0]).astype(o_ref.dtype)


def _norm_plain_kernel(x_ref, g_ref, o_ref):
    x = x_ref[...]
    y = x * lax.rsqrt(jnp.mean(x * x, axis=-1, keepdims=True) + NORM_EPS) * g_ref[...]
    o_ref[...] = y.astype(o_ref.dtype)


def _norm_mod(h, g, scale, shift, seq, tr=256):
    nt, d = h.shape
    tr = min(tr, seq)
    bps = seq // tr
    return pl.pallas_call(
        _norm_mod_kernel,
        grid=(nt // tr,),
        in_specs=[
            pl.BlockSpec((tr, d), lambda i: (i, 0)),
            pl.BlockSpec((1, d), lambda i: (0, 0)),
            pl.BlockSpec((1, 1, d), lambda i: (i // bps, 0, 0)),
            pl.BlockSpec((1, 1, d), lambda i: (i // bps, 0, 0)),
        ],
        out_specs=pl.BlockSpec((tr, d), lambda i: (i, 0)),
        out_shape=jax.ShapeDtypeStruct((nt, d), BF16),
        compiler_params=_params("arbitrary"),
        name="norm_mod",
    )(h, g.reshape(1, d), scale, shift)


def _norm_plain(h, g, tr=256):
    nt, d = h.shape
    tr = min(tr, nt)
    return pl.pallas_call(
        _norm_plain_kernel,
        grid=(nt // tr,),
        in_specs=[pl.BlockSpec((tr, d), lambda i: (i, 0)), pl.BlockSpec((1, d), lambda i: (0, 0))],
        out_specs=pl.BlockSpec((tr, d), lambda i: (i, 0)),
        out_shape=jax.ShapeDtypeStruct((nt, d), F32),
        compiler_params=_params("arbitrary"),
        name="norm_final",
    )(h, g.reshape(1, d))


def _mm_kernel(*refs, n_parts, has_res):
    acc = None
    for x_ref, w_ref in zip(refs[:n_parts], refs[n_parts:2 * n_parts]):
        d = _dot(x_ref[...], w_ref[...])
        acc = d if acc is None else acc + d
    rest = refs[2 * n_parts:]
    if has_res:
        res_ref, gate_ref, o_ref = rest
        o_ref[...] = res_ref[...] + gate_ref[0] * acc
    else:
        (o_ref,) = rest
        o_ref[...] = acc.astype(o_ref.dtype)


def _matmul(xs, w, n_out, col_off, tm, tn, out_dtype, name, res=None, gate=None, seq=None, layer=0):
    m, kp = xs[0].shape
    tm = min(tm, m if seq is None else seq)
    tn = min(tn, n_out)
    assert m % tm == 0 and n_out % tn == 0 and col_off % tn == 0
    cb = col_off // tn
    n_parts = len(xs)
    in_specs = [pl.BlockSpec((tm, kp), lambda i, j: (i, 0)) for _ in xs]
    in_specs += [pl.BlockSpec((None, kp, tn), lambda i, j, p=p: (layer, p, j + cb)) for p in range(n_parts)]
    args = list(xs) + [w] * n_parts
    if res is not None:
        bps = seq // tm
        in_specs += [pl.BlockSpec((tm, tn), lambda i, j: (i, j)),
                     pl.BlockSpec((1, 1, tn), lambda i, j: (i // bps, 0, j))]
        args += [res, gate]
    return pl.pallas_call(
        functools.partial(_mm_kernel, n_parts=n_parts, has_res=res is not None),
        grid=(m // tm, n_out // tn),
        in_specs=in_specs,
        out_specs=pl.BlockSpec((tm, tn), lambda i, j: (i, j)),
        out_shape=jax.ShapeDtypeStruct((m, n_out), out_dtype),
        compiler_params=_params("arbitrary", "arbitrary"),
        name=name,
    )(*args)


HALO = 16


def _ffn_up_kernel(x_ref, xh_ref, wg_ref, wu_ref, cw_ref, cb_ref, o_ref, gs_ref, *, tm, bps):
    i = pl.program_id(0)
    x = x_ref[...]
    wg = wg_ref[...]
    g = _dot(x, wg)
    u = _dot(x, wu_ref[...])
    gh = _dot(xh_ref[...], wg)
    gs_ref[0:HALO, :] = jnp.where(i % bps == 0, 0.0, gh)
    gs_ref[HALO:, :] = g
    g1 = gs_ref[HALO - 1:HALO - 1 + tm, :]
    g2 = gs_ref[HALO - 2:HALO - 2 + tm, :]
    cw = cw_ref[...]
    gc = cw[0:1] * g2 + cw[1:2] * g1 + cw[2:3] * g + cb_ref[...]
    o_ref[...] = (gc * jax.nn.sigmoid(gc) * u).astype(o_ref.dtype)


def _ffn_up(hn, wg, wu, layer, conv_w, conv_b, seq, tm=1024, tn=256):
    m, d = hn.shape
    dff = wg.shape[2]
    tm = min(tm, seq)
    assert dff % tn == 0 and tm % HALO == 0
    bps = seq // tm
    hpb = tm // HALO
    return pl.pallas_call(
        functools.partial(_ffn_up_kernel, tm=tm, bps=bps),
        grid=(m // tm, dff // tn),
        in_specs=[
            pl.BlockSpec((tm, d), lambda i, j: (i, 0)),
            pl.BlockSpec((HALO, d), lambda i, j: (jnp.maximum(i * hpb - 1, 0), 0)),
            pl.BlockSpec((None, d, tn), lambda i, j: (layer, 0, j)),
            pl.BlockSpec((None, d, tn), lambda i, j: (layer, 0, j)),
            pl.BlockSpec((3, tn), lambda i, j: (0, j)),
            pl.BlockSpec((1, tn), lambda i, j: (0, j)),
        ],
        out_specs=pl.BlockSpec((tm, tn), lambda i, j: (i, j)),
        out_shape=jax.ShapeDtypeStruct((m, dff), BF16),
        scratch_shapes=[pltpu.VMEM((tm + HALO, tn), F32)],
        compiler_params=_params("arbitrary", "arbitrary"),
        name="ffn_up",
    )(hn, hn, wg, wu, conv_w, conv_b.reshape(1, dff))


def _swa_kernel(sink_ref, q_ref, kc_ref, kp_ref, vc_ref, vp_ref, cq_ref, sq_ref, cp_ref, sp_ref,
                o_ref, *, n_heads, n_kv, scale):
    i = pl.program_id(1)
    cq, sq, cp, sp = cq_ref[...], sq_ref[...], cp_ref[...], sp_ref[...]
    group = n_heads // n_kv
    qi = lax.broadcasted_iota(jnp.int32, (BLOCK, BLOCK), 0)
    kj = lax.broadcasted_iota(jnp.int32, (BLOCK, BLOCK), 1)
    mask = jnp.concatenate([(kj > qi) & (i > 0), kj <= qi], axis=1)
    for hk in range(n_kv):
        sl = slice(hk * HEAD_DIM, (hk + 1) * HEAD_DIM)
        k2 = jnp.concatenate([_rope(kp_ref[:, sl].astype(F32), cp, sp),
                              _rope(kc_ref[:, sl].astype(F32), cq, sq)], axis=0)
        k2t = k2.T.astype(BF16)
        v2 = jnp.concatenate([vp_ref[:, sl], vc_ref[:, sl]], axis=0)
        heads = [hk * group + g for g in range(group)]
        qs = jnp.concatenate(
            [_rope(q_ref[:, h * HEAD_DIM:(h + 1) * HEAD_DIM].astype(F32), cq, sq).astype(BF16)
             for h in heads], axis=0)
        s_all = _dot(qs, k2t) * scale
        ps, dens = [], []
        for g, h in enumerate(heads):
            s = jnp.where(mask, s_all[g * BLOCK:(g + 1) * BLOCK], NEG)
            sink = sink_ref[h]
            m = jnp.maximum(jnp.max(s, axis=-1, keepdims=True), sink)
            p = jnp.exp(s - m)
            dens.append(jnp.sum(p, axis=-1, keepdims=True) + jnp.exp(sink - m))
            ps.append(p.astype(BF16))
        o_all = _dot(jnp.concatenate(ps, axis=0), v2)
        for g, h in enumerate(heads):
            o_ref[:, h * HEAD_DIM:(h + 1) * HEAD_DIM] = (
                o_all[g * BLOCK:(g + 1) * BLOCK] / dens[g]).astype(o_ref.dtype)


def _swa(proj, cos2, sin2, sinks, batch, seq, n_heads, n_kv):
    nt = proj.shape[0]
    nq = seq // BLOCK
    qw, kw = n_heads * HEAD_DIM, n_kv * HEAD_DIM
    kcol = n_heads // n_kv
    vcol = kcol + 1
    cur = lambda b, i: b * nq + i
    prev = lambda b, i: b * nq + jnp.maximum(i - 1, 0)
    return pl.pallas_call(
        functools.partial(_swa_kernel, n_heads=n_heads, n_kv=n_kv, scale=HEAD_DIM ** -0.5),
        grid=(batch, nq),
        in_specs=[
            pl.BlockSpec(memory_space=pltpu.SMEM),
            pl.BlockSpec((BLOCK, qw), lambda b, i: (cur(b, i), 0)),
            pl.BlockSpec((BLOCK, kw), lambda b, i: (cur(b, i), kcol)),
            pl.BlockSpec((BLOCK, kw), lambda b, i: (prev(b, i), kcol)),
            pl.BlockSpec((BLOCK, kw), lambda b, i: (cur(b, i), vcol)),
            pl.BlockSpec((BLOCK, kw), lambda b, i: (prev(b, i), vcol)),
            pl.BlockSpec((BLOCK, HEAD_DIM), lambda b, i: (cur(b, i), 0)),
            pl.BlockSpec((BLOCK, HEAD_DIM), lambda b, i: (cur(b, i), 0)),
            pl.BlockSpec((BLOCK, HEAD_DIM), lambda b, i: (prev(b, i), 0)),
            pl.BlockSpec((BLOCK, HEAD_DIM), lambda b, i: (prev(b, i), 0)),
        ],
        out_specs=pl.BlockSpec((BLOCK, qw), lambda b, i: (cur(b, i), 0)),
        out_shape=jax.ShapeDtypeStruct((nt, qw), BF16),
        compiler_params=_params("arbitrary", "arbitrary"),
        name="swa_attn",
    )(sinks, proj, proj, proj, proj, proj, cos2, sin2, cos2, sin2)


SB_TILE = 256
SB_HEADS_PER_STEP = 4
SB_DROP_CAP = 104.0


def _sb_kernel(q_ref, kt_ref, v_ref, o_ref, acc_ref, run_ref, *, scale, tile, hp):
    i = pl.program_id(2)
    row = lax.broadcasted_iota(jnp.int32, (tile, tile), 0)
    col = lax.broadcasted_iota(jnp.int32, (tile, tile), 1)
    strict = col < row
    rr = lax.broadcasted_iota(jnp.int32, (tile, tile + HEAD_DIM), 0)
    cc = lax.broadcasted_iota(jnp.int32, (tile, tile + HEAD_DIM), 1)
    cum = ((rr > cc) | (cc >= tile)).astype(BF16)

    def head_step(h, j, diag):
        hs = slice(h * HEAD_DIM, (h + 1) * HEAD_DIM)
        off = pl.multiple_of(j * tile, tile)
        v = v_ref[pl.ds(off, tile), hs]
        z = _dot(q_ref[:, hs], kt_ref[0, h, j]) * scale
        soft = jnp.log(1.0 + jnp.exp(-jnp.abs(z)))
        log_beta = jnp.minimum(z, 0.0) - soft
        drop = jnp.maximum(z, 0.0) + soft
        if diag:
            drop = jnp.where(strict, drop, 0.0)
        hi = drop.astype(BF16)
        lo = (drop - hi.astype(F32)).astype(BF16)
        sums = _dot(hi, cum) + _dot(lo, cum)
        after = sums[:, :tile]
        if not diag:
            run = run_ref[h]
            after = after + jnp.concatenate([run] * (tile // HEAD_DIM), axis=1)
        w = jnp.exp(log_beta - after)
        if diag:
            w = jnp.where(strict, w, 0.0)
        pv = _dot(w.astype(BF16), v)
        if diag:
            acc_ref[h] = pv
            run = sums[:, tile:]
        else:
            acc_ref[h] += pv
            run = run + sums[:, tile:]
        run_ref[h] = run
        return jnp.min(run)

    def step(j, diag):
        least = head_step(0, j, diag)
        for h in range(1, hp):
            least = jnp.minimum(least, head_step(h, j, diag))
        return least

    def cond(st):
        return (st[0] >= 0) & (st[1] < SB_DROP_CAP)

    def body(st):
        return st[0] - 1, step(st[0], False)

    lax.while_loop(cond, body, (i - 1, step(i, True)))
    for h in range(hp):
        o_ref[:, h * HEAD_DIM:(h + 1) * HEAD_DIM] = acc_ref[h].astype(o_ref.dtype)


def _sb(proj, kt, batch, seq, n_heads, qcol, vcol):
    nt = proj.shape[0]
    tile = min(SB_TILE, seq)
    nq = seq // tile
    hp = math.gcd(math.gcd(qcol, vcol), math.gcd(n_heads, SB_HEADS_PER_STEP))
    w = hp * HEAD_DIM
    return pl.pallas_call(
        functools.partial(_sb_kernel, scale=HEAD_DIM ** -0.5, tile=tile, hp=hp),
        grid=(batch, n_heads // hp, nq),
        in_specs=[
            pl.BlockSpec((tile, w), lambda b, h, i: (b * nq + i, qcol // hp + h)),
            pl.BlockSpec((1, hp, nq, HEAD_DIM, tile), lambda b, h, i: (b, h, 0, 0, 0)),
            pl.BlockSpec((seq, w), lambda b, h, i: (b, vcol // hp + h)),
        ],
        out_specs=pl.BlockSpec((tile, w), lambda b, h, i: (b * nq + i, h)),
        out_shape=jax.ShapeDtypeStruct((nt, n_heads * HEAD_DIM), BF16),
        scratch_shapes=[pltpu.VMEM((hp, tile, HEAD_DIM), F32), pltpu.VMEM((hp, tile, HEAD_DIM), F32)],
        compiler_params=_params("arbitrary", "arbitrary", "arbitrary"),
        name="sb_attn",
    )(proj, kt, proj)


def _rope_heads_t_kernel(x_ref, c_ref, s_ref, o_ref, *, n_heads, rotate):
    c, s = c_ref[...], s_ref[...]
    for h in range(n_heads):
        x = x_ref[:, h * HEAD_DIM:(h + 1) * HEAD_DIM].astype(F32)
        if rotate:
            x = _rope(x, c, s)
        o_ref[0, h, 0] = x.T.astype(o_ref.dtype)


def _heads_t(x, first_head, n_heads, cos2, sin2, batch, seq, kw, rotate, name):
    hg = math.gcd(first_head, n_heads)
    n_chunks = seq // kw
    return pl.pallas_call(
        functools.partial(_rope_heads_t_kernel, n_heads=hg, rotate=rotate),
        grid=(batch, n_chunks, n_heads // hg),
        in_specs=[
            pl.BlockSpec((kw, hg * HEAD_DIM), lambda b, c, g: (b * n_chunks + c, first_head // hg + g)),
            pl.BlockSpec((kw, HEAD_DIM), lambda b, c, g: (b * n_chunks + c, 0)),
            pl.BlockSpec((kw, HEAD_DIM), lambda b, c, g: (b * n_chunks + c, 0)),
        ],
        out_specs=pl.BlockSpec((1, hg, 1, HEAD_DIM, kw), lambda b, c, g: (b, g, c, 0, 0)),
        out_shape=jax.ShapeDtypeStruct((batch, n_heads, n_chunks, HEAD_DIM, kw), BF16),
        compiler_params=_params("arbitrary", "arbitrary", "arbitrary"),
        name=name,
    )(x, cos2, sin2)


def _idx_key_kernel(t_ref, g_ref, c_ref, s_ref, o_ref):
    x = t_ref[:, 0:HEAD_DIM]
    y = x * lax.rsqrt(jnp.mean(x * x, axis=-1, keepdims=True) + NORM_EPS) * g_ref[...]
    o_ref[...] = _rope(y, c_ref[...], s_ref[...]).astype(o_ref.dtype)


def _idx_key(tail, g, cos2, sin2, tr=512):
    nt, tw = tail.shape
    tr = min(tr, nt)
    return pl.pallas_call(
        _idx_key_kernel,
        grid=(nt // tr,),
        in_specs=[
            pl.BlockSpec((tr, tw), lambda i: (i, 0)),
            pl.BlockSpec((1, HEAD_DIM), lambda i: (0, 0)),
            pl.BlockSpec((tr, HEAD_DIM), lambda i: (i, 0)),
            pl.BlockSpec((tr, HEAD_DIM), lambda i: (i, 0)),
        ],
        out_specs=pl.BlockSpec((tr, HEAD_DIM), lambda i: (i, 0)),
        out_shape=jax.ShapeDtypeStruct((nt, HEAD_DIM), BF16),
        compiler_params=_params("arbitrary"),
        name="idx_key",
    )(tail, g.reshape(1, HEAD_DIM), cos2, sin2)


def _idx_kernel(qi_ref, kir_ref, tail_ref, c_ref, s_ref, o_ref, qr_ref, wb_ref, sk_ref,
                *, n_ih, kw, topk, idx_scale, n_chunks):
    i = pl.program_id(1)
    cq, sq = c_ref[...], s_ref[...]
    wi = tail_ref[:, HEAD_DIM:2 * HEAD_DIM]
    for h in range(n_ih):
        hs = slice(h * BLOCK, (h + 1) * BLOCK)
        qr_ref[hs, :] = _rope(qi_ref[:, hs].astype(F32), cq, sq).astype(BF16)
        wb_ref[h] = jnp.broadcast_to(wi[:, h:h + 1], (BLOCK, HEAD_DIM))
    nc = ((i + 1) * BLOCK + kw - 1) // kw
    n_lt = kw // HEAD_DIM
    qidx = i * BLOCK + lax.broadcasted_iota(jnp.int32, (BLOCK, kw), 0)
    kloc = lax.broadcasted_iota(jnp.int32, (BLOCK, kw), 1)

    def score_chunk(c, _):
        off = pl.multiple_of(c * kw, kw)
        kc = kir_ref[pl.ds(off, kw), :]
        lg = _dot_nt(qr_ref[...], kc)
        accs = [jnp.zeros((BLOCK, HEAD_DIM), F32) for _ in range(n_lt)]
        for h in range(n_ih):
            w = wb_ref[h]
            for t in range(n_lt):
                tile = lg[h * BLOCK:(h + 1) * BLOCK, t * HEAD_DIM:(t + 1) * HEAD_DIM]
                accs[t] = accs[t] + jnp.maximum(tile, 0.0) * w
        acc = jnp.concatenate(accs, axis=1)
        score = jnp.where(c * kw + kloc <= qidx, acc * idx_scale, -jnp.inf)
        bits = lax.bitcast_convert_type(score, jnp.int32)
        bits = jnp.where(bits == INT_MIN, 0, bits)
        sk_ref[c] = jnp.where(bits < 0, bits ^ 0x7FFFFFFF, bits)
        return 0

    lax.fori_loop(0, nc, score_chunk, 0)

    def count_ge(cand):
        cb = jnp.broadcast_to(cand, (BLOCK, HEAD_DIM))

        def body(c, acc):
            x = sk_ref[c]
            for t in range(n_lt):
                acc = acc + (x[:, t * HEAD_DIM:(t + 1) * HEAD_DIM] >= cb).astype(jnp.int32)
            return acc

        acc = lax.fori_loop(0, nc, body, jnp.zeros((BLOCK, HEAD_DIM), jnp.int32))
        return jnp.sum(acc, axis=-1, keepdims=True)

    zero = jnp.zeros((BLOCK, 1), jnp.int32)
    thr = jnp.where(count_ge(zero) >= topk, zero, INT_MIN)

    def bit_step(t, thr):
        cand = thr | (jnp.int32(1) << (30 - t))
        return jnp.where(count_ge(cand) >= topk, cand, thr)

    thr = lax.fori_loop(0, 31, bit_step, thr)
    tb = jnp.broadcast_to(thr, (BLOCK, kw))
    for c in range(n_chunks):
        @pl.when(c < nc)
        def _():
            sel = (sk_ref[c] >= tb) & (c * kw + kloc <= qidx)
            o_ref[0, c] = jnp.where(sel, 0.0, NEG).astype(o_ref.dtype)

        @pl.when(c >= nc)
        def _():
            o_ref[0, c] = jnp.full((BLOCK, kw), NEG, o_ref.dtype)


def _indexer(qi, kir, tail, cos2, sin2, batch, seq, n_ih, topk, kw):
    nt = qi.shape[0]
    nq = seq // BLOCK
    n_chunks = seq // kw
    idx_scale = (n_ih ** -0.5) * (HEAD_DIM ** -0.5)
    return pl.pallas_call(
        functools.partial(_idx_kernel, n_ih=n_ih, kw=kw, topk=topk, idx_scale=idx_scale,
                          n_chunks=n_chunks),
        grid=(batch, nq),
        in_specs=[
            pl.BlockSpec((BLOCK, n_ih * HEAD_DIM), lambda b, i: (b * nq + i, 0)),
            pl.BlockSpec((seq, HEAD_DIM), lambda b, i: (b, 0)),
            pl.BlockSpec((BLOCK, 2 * HEAD_DIM), lambda b, i: (b * nq + i, 0)),
            pl.BlockSpec((BLOCK, HEAD_DIM), lambda b, i: (b * nq + i, 0)),
            pl.BlockSpec((BLOCK, HEAD_DIM), lambda b, i: (b * nq + i, 0)),
        ],
        out_specs=pl.BlockSpec((1, n_chunks, BLOCK, kw), lambda b, i: (b * nq + i, 0, 0, 0)),
        out_shape=jax.ShapeDtypeStruct((batch * nq, n_chunks, BLOCK, kw), BF16),
        scratch_shapes=[
            pltpu.VMEM((n_ih * BLOCK, HEAD_DIM), BF16),
            pltpu.VMEM((n_ih, BLOCK, HEAD_DIM), F32),
            pltpu.VMEM((n_chunks, BLOCK, kw), jnp.int32),
        ],
        compiler_params=_params("arbitrary", "arbitrary"),
        name="dsa_indexer",
    )(qi, kir, tail, cos2, sin2)


LOG2E = 1.4426950408889634


def _dsa_kernel(q_ref, kt_ref, v_ref, bias_ref, c_ref, s_ref, o_ref, qs_ref, m_ref, l_ref, acc_ref, sc_ref,
                *, group, kw, scale):
    i = pl.program_id(2)
    cq, sq = c_ref[...], s_ref[...]
    for g in range(group):
        qg = _rope(q_ref[:, g * HEAD_DIM:(g + 1) * HEAD_DIM].astype(F32), cq, sq)
        qs_ref[g * BLOCK:(g + 1) * BLOCK, :] = (qg * (scale * LOG2E)).astype(BF16)
        m_ref[g] = jnp.full((BLOCK, HEAD_DIM), NEG, F32)
        l_ref[g] = jnp.zeros((BLOCK, HEAD_DIM), F32)
        acc_ref[g] = jnp.zeros((BLOCK, HEAD_DIM), F32)
    nc = ((i + 1) * BLOCK + kw - 1) // kw
    ones = jnp.ones((kw, HEAD_DIM), BF16)

    def scores(c):
        return _dot(qs_ref[...], kt_ref[0, 0, c])

    def half(c, src, dst):
        s_all = sc_ref[src]
        sc_ref[dst] = scores(jnp.minimum(c + 1, nc - 1))
        off = pl.multiple_of(c * kw, kw)
        vx = jnp.concatenate([v_ref[pl.ds(off, kw), :], ones], axis=1)
        b = bias_ref[0, c].astype(F32)
        ps, alphas = [], []
        for g in range(group):
            s = s_all[g * BLOCK:(g + 1) * BLOCK] + b
            m_old = m_ref[g]
            m_new = jnp.maximum(m_old, jnp.max(s, axis=-1, keepdims=True))
            alphas.append(jnp.exp2(m_old - m_new))
            ps.append(jnp.exp2(s - jnp.concatenate([m_new] * (kw // HEAD_DIM), axis=1)).astype(BF16))
            m_ref[g] = m_new
        pv = _dot(jnp.concatenate(ps, axis=0), vx)
        for g in range(group):
            pg = pv[g * BLOCK:(g + 1) * BLOCK]
            acc_ref[g] = acc_ref[g] * alphas[g] + pg[:, :HEAD_DIM]
            l_ref[g] = l_ref[g] * alphas[g] + pg[:, HEAD_DIM:]

    def body(t, _):
        half(2 * t, 0, 1)

        @pl.when(2 * t + 1 < nc)
        def _():
            half(2 * t + 1, 1, 0)

        return 0

    sc_ref[0] = scores(0)
    lax.fori_loop(0, (nc + 1) // 2, body, 0)
    for g in range(group):
        o_ref[:, g * HEAD_DIM:(g + 1) * HEAD_DIM] = (acc_ref[g] / l_ref[g]).astype(o_ref.dtype)


def _dsa(qkv, kr, bias, cos2, sin2, batch, seq, n_heads, n_kv, kw):
    nt = qkv.shape[0]
    nq = seq // BLOCK
    group = n_heads // n_kv
    n_chunks = seq // kw
    vcol = n_heads + n_kv
    return pl.pallas_call(
        functools.partial(_dsa_kernel, group=group, kw=kw, scale=HEAD_DIM ** -0.5),
        grid=(batch, n_kv, nq),
        in_specs=[
            pl.BlockSpec((BLOCK, group * HEAD_DIM), lambda b, h, i: (b * nq + i, h)),
            pl.BlockSpec((1, 1, n_chunks, HEAD_DIM, kw), lambda b, h, i: (b, h, 0, 0, 0)),
            pl.BlockSpec((seq, HEAD_DIM), lambda b, h, i: (b, vcol + h)),
            pl.BlockSpec((1, n_chunks, BLOCK, kw), lambda b, h, i: (b * nq + i, 0, 0, 0)),
            pl.BlockSpec((BLOCK, HEAD_DIM), lambda b, h, i: (b * nq + i, 0)),
            pl.BlockSpec((BLOCK, HEAD_DIM), lambda b, h, i: (b * nq + i, 0)),
        ],
        out_specs=pl.BlockSpec((BLOCK, group * HEAD_DIM), lambda b, h, i: (b * nq + i, h)),
        out_shape=jax.ShapeDtypeStruct((nt, n_heads * HEAD_DIM), BF16),
        scratch_shapes=[
            pltpu.VMEM((group * BLOCK, HEAD_DIM), BF16),
            pltpu.VMEM((group, BLOCK, HEAD_DIM), F32),
            pltpu.VMEM((group, BLOCK, HEAD_DIM), F32),
            pltpu.VMEM((group, BLOCK, HEAD_DIM), F32),
            pltpu.VMEM((2, group * BLOCK, kw), F32),
        ],
        compiler_params=_params("arbitrary", "arbitrary", "arbitrary"),
        name="dsa_attn",
    )(qkv, kr, qkv, bias, cos2, sin2)


def _rope_tables(positions):
    inv_freq = ROPE_THETA ** (-jnp.arange(0, HEAD_DIM, 2, dtype=F32) / HEAD_DIM)
    ang = positions.astype(F32).reshape(-1)[:, None] * inv_freq
    cos, sin = jnp.cos(ang), jnp.sin(ang)
    return jnp.concatenate([cos, cos], axis=-1), jnp.concatenate([-sin, sin], axis=-1)


def _pick_tile(n, prefs):
    for t in prefs:
        if n % t == 0:
            return t
    return n


def kernel(x, c, positions, norm1_g, norm2_g, ada_w, ada_b, even_w_in, even_sinks, even_w_o,
           odd_w_in, odd_idx_k_g, odd_w_o, ffn_w_gate, ffn_w_up, ffn_conv_w, ffn_conv_b,
           ffn_w_down, final_g):
    batch, seq, d = x.shape
    depth = norm1_g.shape[0]
    nt = batch * seq
    a_heads = even_sinks.shape[1]
    a_kv = max(1, a_heads // 8)
    b_heads = even_w_o.shape[1] // HEAD_DIM - a_heads
    c_heads = odd_w_o.shape[1] // HEAD_DIM
    c_kv = c_heads // 4
    odd_in = odd_w_in.shape[2]
    n_ih = (odd_in - (c_heads + 2 * c_kv) * HEAD_DIM - HEAD_DIM) // (HEAD_DIM + 1)
    topk = min(TOPK_MAX, seq // 4)
    kw = min(512, seq)

    cos2, sin2 = _rope_tables(positions)
    c_pad = jnp.zeros((8, d), F32).at[:batch].set(c)
    mods = _ada(c_pad, ada_w, ada_b, tn=_pick_tile(6 * d, (512, 256, 128)))[:, :batch]
    mods = mods.reshape(depth, batch, 6, 1, d)

    tm = 1024
    w_gate, w_up, w_down = ffn_w_gate.astype(BF16), ffn_w_up.astype(BF16), ffn_w_down.astype(BF16)
    h = x.reshape(nt, d)
    for layer in range(depth):
        shift1, scale1, gate1, shift2, scale2, gate2 = [mods[layer, :, k] for k in range(6)]
        j = layer // 2
        hn = _norm_mod(h, norm1_g[layer], scale1, shift1, seq)
        if layer % 2 == 0:
            w_in = even_w_in.astype(BF16)
            n_in = w_in.shape[2]
            proj = _matmul([hn], w_in, n_in, 0, tm, _pick_tile(n_in, (512, 256, 128)), BF16, "in_proj_even",
                           layer=j)
            oa = _swa(proj, cos2, sin2, even_sinks[j], batch, seq, a_heads, a_kv)
            qcol = a_heads + 2 * a_kv
            kt = _heads_t(proj, qcol + b_heads, b_heads, cos2, sin2, batch, seq, min(SB_TILE, seq), False,
                          "sb_kt")
            ob = _sb(proj, kt, batch, seq, b_heads, qcol, qcol + 2 * b_heads)
            w_o = even_w_o.astype(BF16)
            if a_heads == b_heads:
                parts = [oa, ob]
            else:
                parts = [jnp.concatenate([oa, ob], axis=-1)]
            h = _matmul(parts, w_o, d, 0, tm, _pick_tile(d, (512, 256, 128)), F32, "out_proj_even",
                        res=h, gate=gate1, seq=seq, layer=j)
        else:
            w_in = odd_w_in.astype(BF16)
            n_qkv = (c_heads + 2 * c_kv) * HEAD_DIM
            n_qi = n_ih * HEAD_DIM
            qkv = _matmul([hn], w_in, n_qkv, 0, tm, _pick_tile(n_qkv, (512, 256, 128)), BF16, "in_proj_qkv",
                          layer=j)
            tn_qi = _pick_tile(n_qi, [t for t in (512, 256, 128) if n_qkv % t == 0])
            qi = _matmul([hn], w_in, n_qi, n_qkv, tm, tn_qi, BF16, "in_proj_qi", layer=j)
            w_tail = jnp.zeros((1, d, 2 * HEAD_DIM), BF16).at[0, :, :HEAD_DIM + n_ih].set(
                w_in[j, :, n_qkv + n_qi:])
            tail = _matmul([hn], w_tail, 2 * HEAD_DIM, 0, tm, 2 * HEAD_DIM, F32, "in_proj_idx")
            kr = _heads_t(qkv, c_heads, c_kv, cos2, sin2, batch, seq, kw, True, "rope_k")
            kir = _idx_key(tail, odd_idx_k_g[j], cos2, sin2)
            bias = _indexer(qi, kir, tail, cos2, sin2, batch, seq, n_ih, topk, kw)
            o = _dsa(qkv, kr, bias, cos2, sin2, batch, seq, c_heads, c_kv, kw)
            w_o = odd_w_o.astype(BF16)
            h = _matmul([o], w_o, d, 0, tm, _pick_tile(d, (512, 256, 128)), F32, "out_proj_odd",
                        res=h, gate=gate1, seq=seq, layer=j)
        hn = _norm_mod(h, norm2_g[layer], scale2, shift2, seq)
        a = _ffn_up(hn, w_gate, w_up, layer, ffn_conv_w[layer], ffn_conv_b[layer], seq)
        h = _matmul([a], w_down, d, 0, 512, 256, F32, "ffn_down", res=h, gate=gate2, seq=seq, layer=layer)
    return _norm_plain(h, final_g).reshape(batch, seq, d)
```

```python
import functools
import math

import jax
import jax.numpy as jnp
from jax import lax
from jax.experimental import pallas as pl
from jax.experimental.pallas import tpu as pltpu

F32 = jnp.float32
BF16 = jnp.bfloat16

HEAD_DIM = 128
BLOCK = 128
ROPE_THETA = 10000.0
NORM_EPS = 1e-6
TOPK_MAX = 256
NEG = -1e30
INT_MIN = -(2 ** 31)
VMEM_LIMIT_BYTES = 56 * 1024 * 1024


def _params(*sem):
    return pltpu.CompilerParams(dimension_semantics=sem, vmem_limit_bytes=VMEM_LIMIT_BYTES)


def _dot(a, b):
    return jnp.dot(a, b, preferred_element_type=F32)


def _dot_nt(a, b):
    return lax.dot_general(a, b, (((1,), (1,)), ((), ())), preferred_element_type=F32)


def _rope(x, cos2, sin2):
    return x * cos2 + pltpu.roll(x, HEAD_DIM // 2, axis=1) * sin2


def _ada_kernel(c_ref, w_ref, b_ref, o_ref):
    c = c_ref[...]
    ca = (c * jax.nn.sigmoid(c)).astype(BF16)
    o_ref[0] = _dot(ca, w_ref[0].astype(BF16)) + b_ref[0]


def _ada(c_pad, ada_w, ada_b, tn=512):
    n_layers, d, n = ada_w.shape
    rows = c_pad.shape[0]
    return pl.pallas_call(
        _ada_kernel,
        grid=(n_layers, n // tn),
        in_specs=[
            pl.BlockSpec((rows, d), lambda l, j: (0, 0)),
            pl.BlockSpec((1, d, tn), lambda l, j: (l, 0, j)),
            pl.BlockSpec((1, 1, tn), lambda l, j: (l, 0, j)),
        ],
        out_specs=pl.BlockSpec((1, rows, tn), lambda l, j: (l, 0, j)),
        out_shape=jax.ShapeDtypeStruct((n_layers, rows, n), F32),
        compiler_params=_params("arbitrary", "arbitrary"),
        name="ada_mod",
    )(c_pad, ada_w, ada_b.reshape(n_layers, 1, n))


def _norm_mod_kernel(x_ref, g_ref, sc_ref, sh_ref, o_ref):
    x = x_ref[...]
    y = x * lax.rsqrt(jnp.mean(x * x, axis=-1, keepdims=True) + NORM_EPS) * g_ref[...]
    o_ref[...] = (y * (1.0 + sc_ref[0]) + sh_ref[0]).astype(o_ref.dtype)


def _norm_plain_kernel(x_ref, g_ref, o_ref):
    x = x_ref[...]
    y = x * lax.rsqrt(jnp.mean(x * x, axis=-1, keepdims=True) + NORM_EPS) * g_ref[...]
    o_ref[...] = y.astype(o_ref.dtype)


def _norm_mod(h, g, scale, shift, seq, tr=256):
    nt, d = h.shape
    tr = min(tr, seq)
    bps = seq // tr
    return pl.pallas_call(
        _norm_mod_kernel,
        grid=(nt // tr,),
        in_specs=[
            pl.BlockSpec((tr, d), lambda i: (i, 0)),
            pl.BlockSpec((1, d), lambda i: (0, 0)),
            pl.BlockSpec((1, 1, d), lambda i: (i // bps, 0, 0)),
            pl.BlockSpec((1, 1, d), lambda i: (i // bps, 0, 0)),
        ],
        out_specs=pl.BlockSpec((tr, d), lambda i: (i, 0)),
        out_shape=jax.ShapeDtypeStruct((nt, d), BF16),
        compiler_params=_params("arbitrary"),
        name="norm_mod",
    )(h, g.reshape(1, d), scale, shift)


def _norm_plain(h, g, tr=256):
    nt, d = h.shape
    tr = min(tr, nt)
    return pl.pallas_call(
        _norm_plain_kernel,
        grid=(nt // tr,),
        in_specs=[pl.BlockSpec((tr, d), lambda i: (i, 0)), pl.BlockSpec((1, d), lambda i: (0, 0))],
        out_specs=pl.BlockSpec((tr, d), lambda i: (i, 0)),
        out_shape=jax.ShapeDtypeStruct((nt, d), F32),
        compiler_params=_params("arbitrary"),
        name="norm_final",
    )(h, g.reshape(1, d))


def _mm_kernel(*refs, n_parts, has_res):
    acc = None
    for x_ref, w_ref in zip(refs[:n_parts], refs[n_parts:2 * n_parts]):
        d = _dot(x_ref[...], w_ref[...].astype(BF16))
        acc = d if acc is None else acc + d
    rest = refs[2 * n_parts:]
    if has_res:
        res_ref, gate_ref, o_ref = rest
        o_ref[...] = res_ref[...] + gate_ref[0] * acc
    else:
        (o_ref,) = rest
        o_ref[...] = acc.astype(o_ref.dtype)


def _matmul(xs, w, n_out, col_off, tm, tn, out_dtype, name, res=None, gate=None, seq=None, layer=0):
    m, kp = xs[0].shape
    tm = min(tm, m if seq is None else seq)
    tn = min(tn, n_out)
    assert m % tm == 0 and n_out % tn == 0 and col_off % tn == 0
    cb = col_off // tn
    n_parts = len(xs)
    in_specs = [pl.BlockSpec((tm, kp), lambda i, j: (i, 0)) for _ in xs]
    in_specs += [pl.BlockSpec((None, kp, tn), lambda i, j, p=p: (layer, p, j + cb)) for p in range(n_parts)]
    args = list(xs) + [w] * n_parts
    if res is not None:
        bps = seq // tm
        in_specs += [pl.BlockSpec((tm, tn), lambda i, j: (i, j)),
                     pl.BlockSpec((1, 1, tn), lambda i, j: (i // bps, 0, j))]
        args += [res, gate]
    return pl.pallas_call(
        functools.partial(_mm_kernel, n_parts=n_parts, has_res=res is not None),
        grid=(m // tm, n_out // tn),
        in_specs=in_specs,
        out_specs=pl.BlockSpec((tm, tn), lambda i, j: (i, j)),
        out_shape=jax.ShapeDtypeStruct((m, n_out), out_dtype),
        compiler_params=_params("arbitrary", "arbitrary"),
        name=name,
    )(*args)


HALO = 16
FFN_SLAB = 1024


def _ffn_up_kernel(x_ref, xh_ref, wg_ref, wu_ref, cw_ref, cb_ref, o_ref, gs_ref, *, tm, bps):
    i = pl.program_id(0)
    wg = wg_ref[...].astype(BF16)
    wu = wu_ref[...].astype(BF16)
    cw = cw_ref[...]
    cb = cb_ref[...]
    gs_ref[0:HALO, :] = jnp.where(i % bps == 0, 0.0, _dot(xh_ref[...], wg))
    ts = min(tm, FFN_SLAB)
    for r in range(tm // ts):
        x = x_ref[r * ts:(r + 1) * ts, :]
        g = _dot(x, wg)
        u = _dot(x, wu)
        lo = HALO + r * ts
        gs_ref[lo:lo + ts, :] = g
        g1 = gs_ref[lo - 1:lo - 1 + ts, :]
        g2 = gs_ref[lo - 2:lo - 2 + ts, :]
        gc = cw[0:1] * g2 + cw[1:2] * g1 + cw[2:3] * g + cb
        o_ref[r * ts:(r + 1) * ts, :] = (gc * jax.nn.sigmoid(gc) * u).astype(o_ref.dtype)


def _ffn_up(hn, wg, wu, layer, conv_w, conv_b, seq, tm=1024, tn=256):
    m, d = hn.shape
    dff = wg.shape[2]
    tm = min(tm, seq)
    assert dff % tn == 0 and tm % HALO == 0
    bps = seq // tm
    hpb = tm // HALO
    return pl.pallas_call(
        functools.partial(_ffn_up_kernel, tm=tm, bps=bps),
        grid=(m // tm, dff // tn),
        in_specs=[
            pl.BlockSpec((tm, d), lambda i, j: (i, 0)),
            pl.BlockSpec((HALO, d), lambda i, j: (jnp.maximum(i * hpb - 1, 0), 0)),
            pl.BlockSpec((None, d, tn), lambda i, j: (layer, 0, j)),
            pl.BlockSpec((None, d, tn), lambda i, j: (layer, 0, j)),
            pl.BlockSpec((3, tn), lambda i, j: (0, j)),
            pl.BlockSpec((1, tn), lambda i, j: (0, j)),
        ],
        out_specs=pl.BlockSpec((tm, tn), lambda i, j: (i, j)),
        out_shape=jax.ShapeDtypeStruct((m, dff), BF16),
        scratch_shapes=[pltpu.VMEM((tm + HALO, tn), F32)],
        compiler_params=_params("arbitrary", "arbitrary"),
        name="ffn_up",
    )(hn, hn, wg, wu, conv_w, conv_b.reshape(1, dff))


def _swa_kernel(sink_ref, q_ref, kc_ref, kp_ref, vc_ref, vp_ref, cq_ref, sq_ref, cp_ref, sp_ref,
                o_ref, *, n_heads, n_kv, scale):
    i = pl.program_id(1)
    cq, sq, cp, sp = cq_ref[...], sq_ref[...], cp_ref[...], sp_ref[...]
    group = n_heads // n_kv
    qi = lax.broadcasted_iota(jnp.int32, (BLOCK, BLOCK), 0)
    kj = lax.broadcasted_iota(jnp.int32, (BLOCK, BLOCK), 1)
    mask = jnp.concatenate([(kj > qi) & (i > 0), kj <= qi], axis=1)
    for hk in range(n_kv):
        sl = slice(hk * HEAD_DIM, (hk + 1) * HEAD_DIM)
        k2 = jnp.concatenate([_rope(kp_ref[:, sl].astype(F32), cp, sp),
                              _rope(kc_ref[:, sl].astype(F32), cq, sq)], axis=0)
        k2t = k2.T.astype(BF16)
        v2 = jnp.concatenate([vp_ref[:, sl], vc_ref[:, sl]], axis=0)
        heads = [hk * group + g for g in range(group)]
        qs = jnp.concatenate(
            [_rope(q_ref[:, h * HEAD_DIM:(h + 1) * HEAD_DIM].astype(F32), cq, sq).astype(BF16)
             for h in heads], axis=0)
        s_all = _dot(qs, k2t) * scale
        ps, dens = [], []
        for g, h in enumerate(heads):
            s = jnp.where(mask, s_all[g * BLOCK:(g + 1) * BLOCK], NEG)
            sink = sink_ref[h]
            m = jnp.maximum(jnp.max(s, axis=-1, keepdims=True), sink)
            p = jnp.exp(s - m)
            dens.append(jnp.sum(p, axis=-1, keepdims=True) + jnp.exp(sink - m))
            ps.append(p.astype(BF16))
        o_all = _dot(jnp.concatenate(ps, axis=0), v2)
        for g, h in enumerate(heads):
            o_ref[:, h * HEAD_DIM:(h + 1) * HEAD_DIM] = (
                o_all[g * BLOCK:(g + 1) * BLOCK] / dens[g]).astype(o_ref.dtype)


def _swa(proj, cos2, sin2, sinks, batch, seq, n_heads, n_kv):
    nt = proj.shape[0]
    nq = seq // BLOCK
    qw, kw = n_heads * HEAD_DIM, n_kv * HEAD_DIM
    kcol = n_heads // n_kv
    vcol = kcol + 1
    cur = lambda b, i: b * nq + i
    prev = lambda b, i: b * nq + jnp.maximum(i - 1, 0)
    return pl.pallas_call(
        functools.partial(_swa_kernel, n_heads=n_heads, n_kv=n_kv, scale=HEAD_DIM ** -0.5),
        grid=(batch, nq),
        in_specs=[
            pl.BlockSpec(memory_space=pltpu.SMEM),
            pl.BlockSpec((BLOCK, qw), lambda b, i: (cur(b, i), 0)),
            pl.BlockSpec((BLOCK, kw), lambda b, i: (cur(b, i), kcol)),
            pl.BlockSpec((BLOCK, kw), lambda b, i: (prev(b, i), kcol)),
            pl.BlockSpec((BLOCK, kw), lambda b, i: (cur(b, i), vcol)),
            pl.BlockSpec((BLOCK, kw), lambda b, i: (prev(b, i), vcol)),
            pl.BlockSpec((BLOCK, HEAD_DIM), lambda b, i: (cur(b, i), 0)),
            pl.BlockSpec((BLOCK, HEAD_DIM), lambda b, i: (cur(b, i), 0)),
            pl.BlockSpec((BLOCK, HEAD_DIM), lambda b, i: (prev(b, i), 0)),
            pl.BlockSpec((BLOCK, HEAD_DIM), lambda b, i: (prev(b, i), 0)),
        ],
        out_specs=pl.BlockSpec((BLOCK, qw), lambda b, i: (cur(b, i), 0)),
        out_shape=jax.ShapeDtypeStruct((nt, qw), BF16),
        compiler_params=_params("arbitrary", "arbitrary"),
        name="swa_attn",
    )(sinks, proj, proj, proj, proj, proj, cos2, sin2, cos2, sin2)


SB_TILE = 256
SB_HEADS_PER_STEP = 4
SB_DROP_CAP = 104.0


def _sb_kernel(q_ref, kt_ref, v_ref, o_ref, acc_ref, run_ref, *, scale, tile, hp):
    i = pl.program_id(2)
    row = lax.broadcasted_iota(jnp.int32, (tile, tile), 0)
    col = lax.broadcasted_iota(jnp.int32, (tile, tile), 1)
    strict = col < row
    rr = lax.broadcasted_iota(jnp.int32, (tile, tile + HEAD_DIM), 0)
    cc = lax.broadcasted_iota(jnp.int32, (tile, tile + HEAD_DIM), 1)
    cum = ((rr > cc) | (cc >= tile)).astype(BF16)

    def head_step(h, j, diag):
        hs = slice(h * HEAD_DIM, (h + 1) * HEAD_DIM)
        off = pl.multiple_of(j * tile, tile)
        v = v_ref[pl.ds(off, tile), hs]
        z = _dot(q_ref[:, hs], kt_ref[0, h, j]) * scale
        soft = jnp.log(1.0 + jnp.exp(-jnp.abs(z)))
        log_beta = jnp.minimum(z, 0.0) - soft
        drop = jnp.maximum(z, 0.0) + soft
        if diag:
            drop = jnp.where(strict, drop, 0.0)
        hi = drop.astype(BF16)
        lo = (drop - hi.astype(F32)).astype(BF16)
        sums = _dot(hi, cum) + _dot(lo, cum)
        after = sums[:, :tile]
        if not diag:
            run = run_ref[h]
            after = after + jnp.concatenate([run] * (tile // HEAD_DIM), axis=1)
        w = jnp.exp(log_beta - after)
        if diag:
            w = jnp.where(strict, w, 0.0)
        pv = _dot(w.astype(BF16), v)
        if diag:
            acc_ref[h] = pv
            run = sums[:, tile:]
        else:
            acc_ref[h] += pv
            run = run + sums[:, tile:]
        run_ref[h] = run
        return jnp.min(run)

    def step(j, diag):
        least = head_step(0, j, diag)
        for h in range(1, hp):
            least = jnp.minimum(least, head_step(h, j, diag))
        return least

    def cond(st):
        return (st[0] >= 0) & (st[1] < SB_DROP_CAP)

    def body(st):
        return st[0] - 1, step(st[0], False)

    lax.while_loop(cond, body, (i - 1, step(i, True)))
    for h in range(hp):
        o_ref[:, h * HEAD_DIM:(h + 1) * HEAD_DIM] = acc_ref[h].astype(o_ref.dtype)


def _sb(proj, kt, batch, seq, n_heads, qcol, vcol):
    nt = proj.shape[0]
    tile = min(SB_TILE, seq)
    nq = seq // tile
    hp = math.gcd(math.gcd(qcol, vcol), math.gcd(n_heads, SB_HEADS_PER_STEP))
    w = hp * HEAD_DIM
    return pl.pallas_call(
        functools.partial(_sb_kernel, scale=HEAD_DIM ** -0.5, tile=tile, hp=hp),
        grid=(batch, n_heads // hp, nq),
        in_specs=[
            pl.BlockSpec((tile, w), lambda b, h, i: (b * nq + i, qcol // hp + h)),
            pl.BlockSpec((1, hp, nq, HEAD_DIM, tile), lambda b, h, i: (b, h, 0, 0, 0)),
            pl.BlockSpec((seq, w), lambda b, h, i: (b, vcol // hp + h)),
        ],
        out_specs=pl.BlockSpec((tile, w), lambda b, h, i: (b * nq + i, h)),
        out_shape=jax.ShapeDtypeStruct((nt, n_heads * HEAD_DIM), BF16),
        scratch_shapes=[pltpu.VMEM((hp, tile, HEAD_DIM), F32), pltpu.VMEM((hp, tile, HEAD_DIM), F32)],
        compiler_params=_params("arbitrary", "arbitrary", "arbitrary"),
        name="sb_attn",
    )(proj, kt, proj)


def _rope_heads_t_kernel(x_ref, c_ref, s_ref, o_ref, *, n_heads, rotate):
    c, s = c_ref[...], s_ref[...]
    for h in range(n_heads):
        x = x_ref[:, h * HEAD_DIM:(h + 1) * HEAD_DIM].astype(F32)
        if rotate:
            x = _rope(x, c, s)
        o_ref[0, h, 0] = x.T.astype(o_ref.dtype)


def _heads_t(x, first_head, n_heads, cos2, sin2, batch, seq, kw, rotate, name):
    hg = math.gcd(first_head, n_heads)
    n_chunks = seq // kw
    return pl.pallas_call(
        functools.partial(_rope_heads_t_kernel, n_heads=hg, rotate=rotate),
        grid=(batch, n_chunks, n_heads // hg),
        in_specs=[
            pl.BlockSpec((kw, hg * HEAD_DIM), lambda b, c, g: (b * n_chunks + c, first_head // hg + g)),
            pl.BlockSpec((kw, HEAD_DIM), lambda b, c, g: (b * n_chunks + c, 0)),
            pl.BlockSpec((kw, HEAD_DIM), lambda b, c, g: (b * n_chunks + c, 0)),
        ],
        out_specs=pl.BlockSpec((1, hg, 1, HEAD_DIM, kw), lambda b, c, g: (b, g, c, 0, 0)),
        out_shape=jax.ShapeDtypeStruct((batch, n_heads, n_chunks, HEAD_DIM, kw), BF16),
        compiler_params=_params("arbitrary", "arbitrary", "arbitrary"),
        name=name,
    )(x, cos2, sin2)


def _idx_key_kernel(t_ref, g_ref, c_ref, s_ref, o_ref):
    x = t_ref[:, 0:HEAD_DIM]
    y = x * lax.rsqrt(jnp.mean(x * x, axis=-1, keepdims=True) + NORM_EPS) * g_ref[...]
    o_ref[...] = _rope(y, c_ref[...], s_ref[...]).astype(o_ref.dtype)


def _idx_key(tail, g, cos2, sin2, tr=512):
    nt, tw = tail.shape
    tr = min(tr, nt)
    return pl.pallas_call(
        _idx_key_kernel,
        grid=(nt // tr,),
        in_specs=[
            pl.BlockSpec((tr, tw), lambda i: (i, 0)),
            pl.BlockSpec((1, HEAD_DIM), lambda i: (0, 0)),
            pl.BlockSpec((tr, HEAD_DIM), lambda i: (i, 0)),
            pl.BlockSpec((tr, HEAD_DIM), lambda i: (i, 0)),
        ],
        out_specs=pl.BlockSpec((tr, HEAD_DIM), lambda i: (i, 0)),
        out_shape=jax.ShapeDtypeStruct((nt, HEAD_DIM), BF16),
        compiler_params=_params("arbitrary"),
        name="idx_key",
    )(tail, g.reshape(1, HEAD_DIM), cos2, sin2)


def _idx_kernel(qi_ref, kir_ref, tail_ref, c_ref, s_ref, o_ref, qr_ref, wb_ref, sk_ref, run_ref,
                *, n_ih, kw, topk, idx_scale, n_chunks):
    i = pl.program_id(1)
    cq, sq = c_ref[...], s_ref[...]
    wi = tail_ref[:, HEAD_DIM:2 * HEAD_DIM]
    for h in range(n_ih):
        hs = slice(h * BLOCK, (h + 1) * BLOCK)
        qr_ref[hs, :] = _rope(qi_ref[:, hs].astype(F32), cq, sq).astype(BF16)
        wb_ref[h] = jnp.broadcast_to(wi[:, h:h + 1], (BLOCK, HEAD_DIM))
    nc = ((i + 1) * BLOCK + kw - 1) // kw
    n_lt = kw // HEAD_DIM
    qidx = i * BLOCK + lax.broadcasted_iota(jnp.int32, (BLOCK, kw), 0)
    kloc = lax.broadcasted_iota(jnp.int32, (BLOCK, kw), 1)

    def score_chunk(c, _):
        off = pl.multiple_of(c * kw, kw)
        kc = kir_ref[pl.ds(off, kw), :]
        lg = _dot_nt(qr_ref[...], kc)
        accs = [jnp.zeros((BLOCK, HEAD_DIM), F32) for _ in range(n_lt)]
        for h in range(n_ih):
            w = wb_ref[h]
            for t in range(n_lt):
                tile = lg[h * BLOCK:(h + 1) * BLOCK, t * HEAD_DIM:(t + 1) * HEAD_DIM]
                accs[t] = accs[t] + jnp.maximum(tile, 0.0) * w
        acc = jnp.concatenate(accs, axis=1)
        score = jnp.where(c * kw + kloc <= qidx, acc * idx_scale, -jnp.inf)
        bits = lax.bitcast_convert_type(score, jnp.int32)
        bits = jnp.where(bits == INT_MIN, 0, bits)
        sk_ref[c] = jnp.where(bits < 0, bits ^ 0x7FFFFFFF, bits)
        return 0

    lax.fori_loop(0, nc, score_chunk, 0)

    def count_ge(cand):
        cb = jnp.broadcast_to(cand, (BLOCK, HEAD_DIM))

        def body(c, acc):
            x = sk_ref[c]
            for t in range(n_lt):
                acc = acc + (x[:, t * HEAD_DIM:(t + 1) * HEAD_DIM] >= cb).astype(jnp.int32)
            return acc

        acc = lax.fori_loop(0, nc, body, jnp.zeros((BLOCK, HEAD_DIM), jnp.int32))
        return jnp.sum(acc, axis=-1, keepdims=True)

    zero = jnp.zeros((BLOCK, 1), jnp.int32)
    thr = jnp.where(count_ge(zero) >= topk, zero, INT_MIN)

    def bit_step(t, thr):
        cand = thr | (jnp.int32(1) << (30 - t))
        return jnp.where(count_ge(cand) >= topk, cand, thr)

    thr = lax.fori_loop(0, 31, bit_step, thr)
    tb = jnp.broadcast_to(thr, (BLOCK, kw))
    has_ties = jnp.max(count_ge(thr)) > topk

    @pl.when(jnp.logical_not(has_ties))
    def _():
        for c in range(n_chunks):
            @pl.when(c < nc)
            def _():
                sel = (sk_ref[c] >= tb) & (c * kw + kloc <= qidx)
                o_ref[0, c] = jnp.where(sel, 0.0, NEG).astype(o_ref.dtype)

    @pl.when(has_ties)
    def _():
        need = jnp.broadcast_to(topk - count_ge(thr + 1), (BLOCK, kw)).astype(F32)
        rr = lax.broadcasted_iota(jnp.int32, (kw, kw + HEAD_DIM), 0)
        cc = lax.broadcasted_iota(jnp.int32, (kw, kw + HEAD_DIM), 1)
        before = ((rr < cc) | (cc >= kw)).astype(BF16)
        run_ref[...] = jnp.zeros((BLOCK, HEAD_DIM), F32)
        for c in range(n_chunks):
            @pl.when(c < nc)
            def _():
                x = sk_ref[c]
                causal = c * kw + kloc <= qidx
                tied = (x == tb) & causal
                sums = _dot(jnp.where(tied, 1.0, 0.0).astype(BF16), before)
                run = run_ref[...]
                rank = sums[:, :kw] + jnp.concatenate([run] * n_lt, axis=1)
                sel = ((x > tb) & causal) | (tied & (rank < need))
                o_ref[0, c] = jnp.where(sel, 0.0, NEG).astype(o_ref.dtype)
                run_ref[...] = run + sums[:, kw:]

    for c in range(n_chunks):
        @pl.when(c >= nc)
        def _():
            o_ref[0, c] = jnp.full((BLOCK, kw), NEG, o_ref.dtype)


def _indexer(qi, kir, tail, cos2, sin2, batch, seq, n_ih, topk, kw):
    nt = qi.shape[0]
    nq = seq // BLOCK
    n_chunks = seq // kw
    idx_scale = (n_ih ** -0.5) * (HEAD_DIM ** -0.5)
    return pl.pallas_call(
        functools.partial(_idx_kernel, n_ih=n_ih, kw=kw, topk=topk, idx_scale=idx_scale,
                          n_chunks=n_chunks),
        grid=(batch, nq),
        in_specs=[
            pl.BlockSpec((BLOCK, n_ih * HEAD_DIM), lambda b, i: (b * nq + i, 0)),
            pl.BlockSpec((seq, HEAD_DIM), lambda b, i: (b, 0)),
            pl.BlockSpec((BLOCK, 2 * HEAD_DIM), lambda b, i: (b * nq + i, 0)),
            pl.BlockSpec((BLOCK, HEAD_DIM), lambda b, i: (b * nq + i, 0)),
            pl.BlockSpec((BLOCK, HEAD_DIM), lambda b, i: (b * nq + i, 0)),
        ],
        out_specs=pl.BlockSpec((1, n_chunks, BLOCK, kw), lambda b, i: (b * nq + i, 0, 0, 0)),
        out_shape=jax.ShapeDtypeStruct((batch * nq, n_chunks, BLOCK, kw), BF16),
        scratch_shapes=[
            pltpu.VMEM((n_ih * BLOCK, HEAD_DIM), BF16),
            pltpu.VMEM((n_ih, BLOCK, HEAD_DIM), F32),
            pltpu.VMEM((n_chunks, BLOCK, kw), jnp.int32),
            pltpu.VMEM((BLOCK, HEAD_DIM), F32),
        ],
        compiler_params=_params("arbitrary", "arbitrary"),
        name="dsa_indexer",
    )(qi, kir, tail, cos2, sin2)


LOG2E = 1.4426950408889634


def _dsa_kernel(q_ref, kt_ref, v_ref, bias_ref, c_ref, s_ref, o_ref, qs_ref, m_ref, l_ref, acc_ref, sc_ref,
                *, group, qb, kw, scale):
    i = pl.program_id(2)
    n_rg = group * qb
    for g in range(group):
        for t in range(qb):
            r = g * qb + t
            rows = slice(t * BLOCK, (t + 1) * BLOCK)
            qg = _rope(q_ref[rows, g * HEAD_DIM:(g + 1) * HEAD_DIM].astype(F32), c_ref[rows, :], s_ref[rows, :])
            qs_ref[r * BLOCK:(r + 1) * BLOCK, :] = (qg * (scale * LOG2E)).astype(BF16)
            m_ref[r] = jnp.full((BLOCK, HEAD_DIM), NEG, F32)
            l_ref[r] = jnp.zeros((BLOCK, HEAD_DIM), F32)
            acc_ref[r] = jnp.zeros((BLOCK, HEAD_DIM), F32)
    nc = ((i + 1) * qb * BLOCK + kw - 1) // kw
    ones = jnp.ones((kw, HEAD_DIM), BF16)

    def scores(c):
        return _dot(qs_ref[...], kt_ref[0, 0, c])

    def half(c, src, dst):
        s_all = sc_ref[src]
        sc_ref[dst] = scores(jnp.minimum(c + 1, nc - 1))
        off = pl.multiple_of(c * kw, kw)
        vx = jnp.concatenate([v_ref[pl.ds(off, kw), :], ones], axis=1)
        bs = [bias_ref[t, c].astype(F32) for t in range(qb)]
        ps, alphas = [], []
        for r in range(n_rg):
            s = s_all[r * BLOCK:(r + 1) * BLOCK] + bs[r % qb]
            m_old = m_ref[r]
            m_new = jnp.maximum(m_old, jnp.max(s, axis=-1, keepdims=True))
            alphas.append(jnp.exp2(m_old - m_new))
            ps.append(jnp.exp2(s - jnp.concatenate([m_new] * (kw // HEAD_DIM), axis=1)).astype(BF16))
            m_ref[r] = m_new
        pv = _dot(jnp.concatenate(ps, axis=0), vx)
        for r in range(n_rg):
            pr = pv[r * BLOCK:(r + 1) * BLOCK]
            acc_ref[r] = acc_ref[r] * alphas[r] + pr[:, :HEAD_DIM]
            l_ref[r] = l_ref[r] * alphas[r] + pr[:, HEAD_DIM:]

    def body(t, _):
        half(2 * t, 0, 1)

        @pl.when(2 * t + 1 < nc)
        def _():
            half(2 * t + 1, 1, 0)

        return 0

    sc_ref[0] = scores(0)
    lax.fori_loop(0, (nc + 1) // 2, body, 0)
    for g in range(group):
        for t in range(qb):
            r = g * qb + t
            o_ref[t * BLOCK:(t + 1) * BLOCK, g * HEAD_DIM:(g + 1) * HEAD_DIM] = (
                acc_ref[r] / l_ref[r]).astype(o_ref.dtype)


DSA_Q_BLOCKS = 1


def _dsa(qkv, kr, bias, cos2, sin2, batch, seq, n_heads, n_kv, kw):
    nt = qkv.shape[0]
    qb = DSA_Q_BLOCKS if (seq // BLOCK) % DSA_Q_BLOCKS == 0 else 1
    tq = qb * BLOCK
    nq = seq // tq
    group = n_heads // n_kv
    n_rg = group * qb
    n_chunks = seq // kw
    vcol = n_heads + n_kv
    return pl.pallas_call(
        functools.partial(_dsa_kernel, group=group, qb=qb, kw=kw, scale=HEAD_DIM ** -0.5),
        grid=(batch, n_kv, nq),
        in_specs=[
            pl.BlockSpec((tq, group * HEAD_DIM), lambda b, h, i: (b * nq + i, h)),
            pl.BlockSpec((1, 1, n_chunks, HEAD_DIM, kw), lambda b, h, i: (b, h, 0, 0, 0)),
            pl.BlockSpec((seq, HEAD_DIM), lambda b, h, i: (b, vcol + h)),
            pl.BlockSpec((qb, n_chunks, BLOCK, kw), lambda b, h, i: (b * nq + i, 0, 0, 0)),
            pl.BlockSpec((tq, HEAD_DIM), lambda b, h, i: (b * nq + i, 0)),
            pl.BlockSpec((tq, HEAD_DIM), lambda b, h, i: (b * nq + i, 0)),
        ],
        out_specs=pl.BlockSpec((tq, group * HEAD_DIM), lambda b, h, i: (b * nq + i, h)),
        out_shape=jax.ShapeDtypeStruct((nt, n_heads * HEAD_DIM), BF16),
        scratch_shapes=[
            pltpu.VMEM((n_rg * BLOCK, HEAD_DIM), BF16),
            pltpu.VMEM((n_rg, BLOCK, HEAD_DIM), F32),
            pltpu.VMEM((n_rg, BLOCK, HEAD_DIM), F32),
            pltpu.VMEM((n_rg, BLOCK, HEAD_DIM), F32),
            pltpu.VMEM((2, n_rg * BLOCK, kw), F32),
        ],
        compiler_params=_params("arbitrary", "arbitrary", "arbitrary"),
        name="dsa_attn",
    )(qkv, kr, qkv, bias, cos2, sin2)


def _rope_tables(positions):
    inv_freq = ROPE_THETA ** (-jnp.arange(0, HEAD_DIM, 2, dtype=F32) / HEAD_DIM)
    ang = positions.astype(F32).reshape(-1)[:, None] * inv_freq
    cos, sin = jnp.cos(ang), jnp.sin(ang)
    return jnp.concatenate([cos, cos], axis=-1), jnp.concatenate([-sin, sin], axis=-1)


def _pick_tile(n, prefs):
    for t in prefs:
        if n % t == 0:
            return t
    return n


def kernel(x, c, positions, norm1_g, norm2_g, ada_w, ada_b, even_w_in, even_sinks, even_w_o,
           odd_w_in, odd_idx_k_g, odd_w_o, ffn_w_gate, ffn_w_up, ffn_conv_w, ffn_conv_b,
           ffn_w_down, final_g):
    batch, seq, d = x.shape
    depth = norm1_g.shape[0]
    nt = batch * seq
    a_heads = even_sinks.shape[1]
    a_kv = max(1, a_heads // 8)
    b_heads = even_w_o.shape[1] // HEAD_DIM - a_heads
    c_heads = odd_w_o.shape[1] // HEAD_DIM
    c_kv = c_heads // 4
    odd_in = odd_w_in.shape[2]
    n_ih = (odd_in - (c_heads + 2 * c_kv) * HEAD_DIM - HEAD_DIM) // (HEAD_DIM + 1)
    topk = min(TOPK_MAX, seq // 4)
    kw = min(512, seq)

    cos2, sin2 = _rope_tables(positions)
    c_pad = jnp.zeros((8, d), F32).at[:batch].set(c)
    mods = _ada(c_pad, ada_w, ada_b, tn=_pick_tile(6 * d, (512, 256, 128)))[:, :batch]
    mods = mods.reshape(depth, batch, 6, 1, d)

    tm = 1024
    w_down = ffn_w_down.astype(BF16)
    h = x.reshape(nt, d)
    for layer in range(depth):
        shift1, scale1, gate1, shift2, scale2, gate2 = [mods[layer, :, k] for k in range(6)]
        j = layer // 2
        hn = _norm_mod(h, norm1_g[layer], scale1, shift1, seq)
        if layer % 2 == 0:
            w_in = even_w_in
            n_in = w_in.shape[2]
            proj = _matmul([hn], w_in, n_in, 0, tm, _pick_tile(n_in, (512, 256, 128)), BF16, "in_proj_even",
                           layer=j)
            oa = _swa(proj, cos2, sin2, even_sinks[j], batch, seq, a_heads, a_kv)
            qcol = a_heads + 2 * a_kv
            kt = _heads_t(proj, qcol + b_heads, b_heads, cos2, sin2, batch, seq, min(SB_TILE, seq), False,
                          "sb_kt")
            ob = _sb(proj, kt, batch, seq, b_heads, qcol, qcol + 2 * b_heads)
            w_o = even_w_o
            if a_heads == b_heads:
                parts = [oa, ob]
            else:
                parts = [jnp.concatenate([oa, ob], axis=-1)]
            h = _matmul(parts, w_o, d, 0, tm, _pick_tile(d, (512, 256, 128)), F32, "out_proj_even",
                        res=h, gate=gate1, seq=seq, layer=j)
        else:
            w_in = odd_w_in
            n_qkv = (c_heads + 2 * c_kv) * HEAD_DIM
            n_qi = n_ih * HEAD_DIM
            qkv = _matmul([hn], w_in, n_qkv, 0, tm, _pick_tile(n_qkv, (512, 256, 128)), BF16, "in_proj_qkv",
                          layer=j)
            tn_qi = _pick_tile(n_qi, [t for t in (512, 256, 128) if n_qkv % t == 0])
            qi = _matmul([hn], w_in, n_qi, n_qkv, tm, tn_qi, BF16, "in_proj_qi", layer=j)
            w_tail = jnp.zeros((1, d, 2 * HEAD_DIM), BF16).at[0, :, :HEAD_DIM + n_ih].set(
                w_in[j, :, n_qkv + n_qi:].astype(BF16))
            tail = _matmul([hn], w_tail, 2 * HEAD_DIM, 0, tm, 2 * HEAD_DIM, F32, "in_proj_idx")
            kr = _heads_t(qkv, c_heads, c_kv, cos2, sin2, batch, seq, kw, True, "rope_k")
            kir = _idx_key(tail, odd_idx_k_g[j], cos2, sin2)
            bias = _indexer(qi, kir, tail, cos2, sin2, batch, seq, n_ih, topk, kw)
            o = _dsa(qkv, kr, bias, cos2, sin2, batch, seq, c_heads, c_kv, kw)
            w_o = odd_w_o
            h = _matmul([o], w_o, d, 0, tm, _pick_tile(d, (512, 256, 128)), F32, "out_proj_odd",
                        res=h, gate=gate1, seq=seq, layer=j)
        hn = _norm_mod(h, norm2_g[layer], scale2, shift2, seq)
        a = _ffn_up(hn, ffn_w_gate, ffn_w_up, layer, ffn_conv_w[layer], ffn_conv_b[layer], seq)
        h = _matmul([a], w_down, d, 0, 512, 256, F32, "ffn_down", res=h, gate=gate2, seq=seq, layer=layer)
    return _norm_plain(h, final_g).reshape(batch, seq, d)
```

```python
import functools
import math

import jax
import jax.numpy as jnp
from jax import lax
from jax.experimental import pallas as pl
from jax.experimental.pallas import tpu as pltpu

F32 = jnp.float32
BF16 = jnp.bfloat16

HEAD_DIM = 128
BLOCK = 128
ROPE_THETA = 10000.0
NORM_EPS = 1e-6
TOPK_MAX = 256
NEG = -1e30
INT_MIN = -(2 ** 31)
VMEM_LIMIT_BYTES = 56 * 1024 * 1024


def _params(*sem):
    return pltpu.CompilerParams(dimension_semantics=sem, vmem_limit_bytes=VMEM_LIMIT_BYTES)


def _dot(a, b):
    return jnp.dot(a, b, preferred_element_type=F32)


def _dot_nt(a, b):
    return lax.dot_general(a, b, (((1,), (1,)), ((), ())), preferred_element_type=F32)


def _rope(x, cos2, sin2):
    return x * cos2 + pltpu.roll(x, HEAD_DIM // 2, axis=1) * sin2


def _ada_kernel(c_ref, w_ref, b_ref, o_ref):
    c = c_ref[...]
    ca = (c * jax.nn.sigmoid(c)).astype(BF16)
    o_ref[0] = _dot(ca, w_ref[0].astype(BF16)) + b_ref[0]


def _ada(c_pad, ada_w, ada_b, tn=512):
    n_layers, d, n = ada_w.shape
    rows = c_pad.shape[0]
    return pl.pallas_call(
        _ada_kernel,
        grid=(n_layers, n // tn),
        in_specs=[
            pl.BlockSpec((rows, d), lambda l, j: (0, 0)),
            pl.BlockSpec((1, d, tn), lambda l, j: (l, 0, j)),
            pl.BlockSpec((1, 1, tn), lambda l, j: (l, 0, j)),
        ],
        out_specs=pl.BlockSpec((1, rows, tn), lambda l, j: (l, 0, j)),
        out_shape=jax.ShapeDtypeStruct((n_layers, rows, n), F32),
        compiler_params=_params("arbitrary", "arbitrary"),
        name="ada_mod",
    )(c_pad, ada_w, ada_b.reshape(n_layers, 1, n))


def _norm_mod_kernel(x_ref, g_ref, sc_ref, sh_ref, o_ref):
    x = x_ref[...]
    y = x * lax.rsqrt(jnp.mean(x * x, axis=-1, keepdims=True) + NORM_EPS) * g_ref[...]
    o_ref[...] = (y * (1.0 + sc_ref[0]) + sh_ref[0]).astype(o_ref.dtype)


def _norm_plain_kernel(x_ref, g_ref, o_ref):
    x = x_ref[...]
    y = x * lax.rsqrt(jnp.mean(x * x, axis=-1, keepdims=True) + NORM_EPS) * g_ref[...]
    o_ref[...] = y.astype(o_ref.dtype)


def _norm_mod(h, g, scale, shift, seq, tr=512):
    nt, d = h.shape
    tr = min(tr, seq)
    bps = seq // tr
    return pl.pallas_call(
        _norm_mod_kernel,
        grid=(nt // tr,),
        in_specs=[
            pl.BlockSpec((tr, d), lambda i: (i, 0)),
            pl.BlockSpec((1, d), lambda i: (0, 0)),
            pl.BlockSpec((1, 1, d), lambda i: (i // bps, 0, 0)),
            pl.BlockSpec((1, 1, d), lambda i: (i // bps, 0, 0)),
        ],
        out_specs=pl.BlockSpec((tr, d), lambda i: (i, 0)),
        out_shape=jax.ShapeDtypeStruct((nt, d), BF16),
        compiler_params=_params("arbitrary"),
        name="norm_mod",
    )(h, g.reshape(1, d), scale, shift)


def _norm_plain(h, g, tr=512):
    nt, d = h.shape
    tr = min(tr, nt)
    return pl.pallas_call(
        _norm_plain_kernel,
        grid=(nt // tr,),
        in_specs=[pl.BlockSpec((tr, d), lambda i: (i, 0)), pl.BlockSpec((1, d), lambda i: (0, 0))],
        out_specs=pl.BlockSpec((tr, d), lambda i: (i, 0)),
        out_shape=jax.ShapeDtypeStruct((nt, d), F32),
        compiler_params=_params("arbitrary"),
        name="norm_final",
    )(h, g.reshape(1, d))


def _mm_kernel(*refs, n_parts, has_res):
    acc = None
    for x_ref, w_ref in zip(refs[:n_parts], refs[n_parts:2 * n_parts]):
        d = _dot(x_ref[...], w_ref[...].astype(BF16))
        acc = d if acc is None else acc + d
    rest = refs[2 * n_parts:]
    if has_res:
        res_ref, gate_ref, o_ref = rest
        o_ref[...] = res_ref[...] + gate_ref[0] * acc
    else:
        (o_ref,) = rest
        o_ref[...] = acc.astype(o_ref.dtype)


def _matmul(xs, w, n_out, col_off, tm, tn, out_dtype, name, res=None, gate=None, seq=None, layer=0,
            single_buffer_x=False):
    m, kp = xs[0].shape
    tm = min(tm, m if seq is None else seq)
    tn = min(tn, n_out)
    assert m % tm == 0 and n_out % tn == 0 and col_off % tn == 0
    cb = col_off // tn
    n_parts = len(xs)
    x_mode = dict(pipeline_mode=pl.Buffered(1)) if single_buffer_x else {}
    in_specs = [pl.BlockSpec((tm, kp), lambda i, j: (i, 0), **x_mode) for _ in xs]
    in_specs += [pl.BlockSpec((None, kp, tn), lambda i, j, p=p: (layer, p, j + cb)) for p in range(n_parts)]
    args = list(xs) + [w] * n_parts
    if res is not None:
        bps = seq // tm
        in_specs += [pl.BlockSpec((tm, tn), lambda i, j: (i, j)),
                     pl.BlockSpec((1, 1, tn), lambda i, j: (i // bps, 0, j))]
        args += [res, gate]
    return pl.pallas_call(
        functools.partial(_mm_kernel, n_parts=n_parts, has_res=res is not None),
        grid=(m // tm, n_out // tn),
        in_specs=in_specs,
        out_specs=pl.BlockSpec((tm, tn), lambda i, j: (i, j)),
        out_shape=jax.ShapeDtypeStruct((m, n_out), out_dtype),
        compiler_params=_params("arbitrary", "arbitrary"),
        name=name,
    )(*args)


HALO = 16


def _ffn_up_kernel(x_ref, xh_ref, wg_ref, wu_ref, cw_ref, cb_ref, o_ref, gs_ref, *, tm, bps):
    i = pl.program_id(0)
    x = x_ref[...]
    wg = wg_ref[...].astype(BF16)
    g = _dot(x, wg)
    u = _dot(x, wu_ref[...].astype(BF16))
    gs_ref[0:HALO, :] = jnp.where(i % bps == 0, 0.0, _dot(xh_ref[...], wg))
    gs_ref[HALO:, :] = g
    g1 = gs_ref[HALO - 1:HALO - 1 + tm, :]
    g2 = gs_ref[HALO - 2:HALO - 2 + tm, :]
    cw = cw_ref[...]
    gc = cw[0:1] * g2 + cw[1:2] * g1 + cw[2:3] * g + cb_ref[...]
    o_ref[...] = (gc * jax.nn.sigmoid(gc) * u).astype(o_ref.dtype)


def _ffn_up(hn, wg, wu, layer, conv_w, conv_b, seq, tm=1024, tn=256):
    m, d = hn.shape
    dff = wg.shape[2]
    tm = min(tm, seq)
    assert dff % tn == 0 and tm % HALO == 0
    bps = seq // tm
    hpb = tm // HALO
    return pl.pallas_call(
        functools.partial(_ffn_up_kernel, tm=tm, bps=bps),
        grid=(m // tm, dff // tn),
        in_specs=[
            pl.BlockSpec((tm, d), lambda i, j: (i, 0)),
            pl.BlockSpec((HALO, d), lambda i, j: (jnp.maximum(i * hpb - 1, 0), 0)),
            pl.BlockSpec((None, d, tn), lambda i, j: (layer, 0, j)),
            pl.BlockSpec((None, d, tn), lambda i, j: (layer, 0, j)),
            pl.BlockSpec((3, tn), lambda i, j: (0, j)),
            pl.BlockSpec((1, tn), lambda i, j: (0, j)),
        ],
        out_specs=pl.BlockSpec((tm, tn), lambda i, j: (i, j)),
        out_shape=jax.ShapeDtypeStruct((m, dff), BF16),
        scratch_shapes=[pltpu.VMEM((tm + HALO, tn), F32)],
        compiler_params=_params("arbitrary", "arbitrary"),
        name="ffn_up",
    )(hn, hn, wg, wu, conv_w, conv_b.reshape(1, dff))


def _swa_kernel(sink_ref, q_ref, kc_ref, kp_ref, vc_ref, vp_ref, cq_ref, sq_ref, cp_ref, sp_ref,
                o_ref, *, n_heads, n_kv, scale):
    i = pl.program_id(1)
    cq, sq, cp, sp = cq_ref[...], sq_ref[...], cp_ref[...], sp_ref[...]
    group = n_heads // n_kv
    qi = lax.broadcasted_iota(jnp.int32, (BLOCK, BLOCK), 0)
    kj = lax.broadcasted_iota(jnp.int32, (BLOCK, BLOCK), 1)
    mask = jnp.concatenate([(kj > qi) & (i > 0), kj <= qi], axis=1)
    for hk in range(n_kv):
        sl = slice(hk * HEAD_DIM, (hk + 1) * HEAD_DIM)
        k2 = jnp.concatenate([_rope(kp_ref[:, sl].astype(F32), cp, sp),
                              _rope(kc_ref[:, sl].astype(F32), cq, sq)], axis=0)
        k2t = k2.T.astype(BF16)
        v2 = jnp.concatenate([vp_ref[:, sl], vc_ref[:, sl]], axis=0)
        heads = [hk * group + g for g in range(group)]
        qs = jnp.concatenate(
            [_rope(q_ref[:, h * HEAD_DIM:(h + 1) * HEAD_DIM].astype(F32), cq, sq).astype(BF16)
             for h in heads], axis=0)
        s_all = _dot(qs, k2t) * scale
        ps, dens = [], []
        for g, h in enumerate(heads):
            s = jnp.where(mask, s_all[g * BLOCK:(g + 1) * BLOCK], NEG)
            sink = sink_ref[h]
            m = jnp.maximum(jnp.max(s, axis=-1, keepdims=True), sink)
            p = jnp.exp(s - m)
            dens.append(jnp.sum(p, axis=-1, keepdims=True) + jnp.exp(sink - m))
            ps.append(p.astype(BF16))
        o_all = _dot(jnp.concatenate(ps, axis=0), v2)
        for g, h in enumerate(heads):
            o_ref[:, h * HEAD_DIM:(h + 1) * HEAD_DIM] = (
                o_all[g * BLOCK:(g + 1) * BLOCK] / dens[g]).astype(o_ref.dtype)


def _swa(proj, cos2, sin2, sinks, batch, seq, n_heads, n_kv):
    nt = proj.shape[0]
    nq = seq // BLOCK
    qw, kw = n_heads * HEAD_DIM, n_kv * HEAD_DIM
    kcol = n_heads // n_kv
    vcol = kcol + 1
    cur = lambda b, i: b * nq + i
    prev = lambda b, i: b * nq + jnp.maximum(i - 1, 0)
    return pl.pallas_call(
        functools.partial(_swa_kernel, n_heads=n_heads, n_kv=n_kv, scale=HEAD_DIM ** -0.5),
        grid=(batch, nq),
        in_specs=[
            pl.BlockSpec(memory_space=pltpu.SMEM),
            pl.BlockSpec((BLOCK, qw), lambda b, i: (cur(b, i), 0)),
            pl.BlockSpec((BLOCK, kw), lambda b, i: (cur(b, i), kcol)),
            pl.BlockSpec((BLOCK, kw), lambda b, i: (prev(b, i), kcol)),
            pl.BlockSpec((BLOCK, kw), lambda b, i: (cur(b, i), vcol)),
            pl.BlockSpec((BLOCK, kw), lambda b, i: (prev(b, i), vcol)),
            pl.BlockSpec((BLOCK, HEAD_DIM), lambda b, i: (cur(b, i), 0)),
            pl.BlockSpec((BLOCK, HEAD_DIM), lambda b, i: (cur(b, i), 0)),
            pl.BlockSpec((BLOCK, HEAD_DIM), lambda b, i: (prev(b, i), 0)),
            pl.BlockSpec((BLOCK, HEAD_DIM), lambda b, i: (prev(b, i), 0)),
        ],
        out_specs=pl.BlockSpec((BLOCK, qw), lambda b, i: (cur(b, i), 0)),
        out_shape=jax.ShapeDtypeStruct((nt, qw), BF16),
        compiler_params=_params("arbitrary", "arbitrary"),
        name="swa_attn",
    )(sinks, proj, proj, proj, proj, proj, cos2, sin2, cos2, sin2)


SB_TILE = 256
SB_HEADS_PER_STEP = 4
SB_DROP_CAP = 104.0


def _sb_kernel(q_ref, kt_ref, v_ref, o_ref, acc_ref, run_ref, *, scale, tile, hp):
    i = pl.program_id(2)
    row = lax.broadcasted_iota(jnp.int32, (tile, tile), 0)
    col = lax.broadcasted_iota(jnp.int32, (tile, tile), 1)
    strict = col < row
    rr = lax.broadcasted_iota(jnp.int32, (tile, tile + HEAD_DIM), 0)
    cc = lax.broadcasted_iota(jnp.int32, (tile, tile + HEAD_DIM), 1)
    cum = ((rr > cc) | (cc >= tile)).astype(BF16)

    def head_step(h, j, diag):
        hs = slice(h * HEAD_DIM, (h + 1) * HEAD_DIM)
        off = pl.multiple_of(j * tile, tile)
        v = v_ref[pl.ds(off, tile), hs]
        z = _dot(q_ref[:, hs], kt_ref[0, h, j]) * scale
        soft = jnp.log(1.0 + jnp.exp(-jnp.abs(z)))
        log_beta = jnp.minimum(z, 0.0) - soft
        drop = jnp.maximum(z, 0.0) + soft
        if diag:
            drop = jnp.where(strict, drop, 0.0)
        hi = drop.astype(BF16)
        lo = (drop - hi.astype(F32)).astype(BF16)
        sums = _dot(hi, cum) + _dot(lo, cum)
        after = sums[:, :tile]
        if not diag:
            run = run_ref[h]
            after = after + jnp.concatenate([run] * (tile // HEAD_DIM), axis=1)
        w = jnp.exp(log_beta - after)
        if diag:
            w = jnp.where(strict, w, 0.0)
        pv = _dot(w.astype(BF16), v)
        if diag:
            acc_ref[h] = pv
            run = sums[:, tile:]
        else:
            acc_ref[h] += pv
            run = run + sums[:, tile:]
        run_ref[h] = run
        return jnp.min(run)

    def step(j, diag):
        least = head_step(0, j, diag)
        for h in range(1, hp):
            least = jnp.minimum(least, head_step(h, j, diag))
        return least

    def cond(st):
        return (st[0] >= 0) & (st[1] < SB_DROP_CAP)

    def body(st):
        return st[0] - 1, step(st[0], False)

    lax.while_loop(cond, body, (i - 1, step(i, True)))
    for h in range(hp):
        o_ref[:, h * HEAD_DIM:(h + 1) * HEAD_DIM] = acc_ref[h].astype(o_ref.dtype)


def _sb(proj, kt, batch, seq, n_heads, qcol, vcol):
    nt = proj.shape[0]
    tile = min(SB_TILE, seq)
    nq = seq // tile
    hp = math.gcd(math.gcd(qcol, vcol), math.gcd(n_heads, SB_HEADS_PER_STEP))
    w = hp * HEAD_DIM
    return pl.pallas_call(
        functools.partial(_sb_kernel, scale=HEAD_DIM ** -0.5, tile=tile, hp=hp),
        grid=(batch, n_heads // hp, nq),
        in_specs=[
            pl.BlockSpec((tile, w), lambda b, h, i: (b * nq + i, qcol // hp + h)),
            pl.BlockSpec((1, hp, nq, HEAD_DIM, tile), lambda b, h, i: (b, h, 0, 0, 0)),
            pl.BlockSpec((seq, w), lambda b, h, i: (b, vcol // hp + h)),
        ],
        out_specs=pl.BlockSpec((tile, w), lambda b, h, i: (b * nq + i, h)),
        out_shape=jax.ShapeDtypeStruct((nt, n_heads * HEAD_DIM), BF16),
        scratch_shapes=[pltpu.VMEM((hp, tile, HEAD_DIM), F32), pltpu.VMEM((hp, tile, HEAD_DIM), F32)],
        compiler_params=_params("arbitrary", "arbitrary", "arbitrary"),
        name="sb_attn",
    )(proj, kt, proj)


def _rope_heads_t_kernel(x_ref, c_ref, s_ref, o_ref, *, n_heads, rotate):
    c, s = c_ref[...], s_ref[...]
    for h in range(n_heads):
        x = x_ref[:, h * HEAD_DIM:(h + 1) * HEAD_DIM].astype(F32)
        if rotate:
            x = _rope(x, c, s)
        o_ref[0, h, 0] = x.T.astype(o_ref.dtype)


def _heads_t(x, first_head, n_heads, cos2, sin2, batch, seq, kw, rotate, name):
    hg = math.gcd(first_head, n_heads)
    n_chunks = seq // kw
    return pl.pallas_call(
        functools.partial(_rope_heads_t_kernel, n_heads=hg, rotate=rotate),
        grid=(batch, n_chunks, n_heads // hg),
        in_specs=[
            pl.BlockSpec((kw, hg * HEAD_DIM), lambda b, c, g: (b * n_chunks + c, first_head // hg + g)),
            pl.BlockSpec((kw, HEAD_DIM), lambda b, c, g: (b * n_chunks + c, 0)),
            pl.BlockSpec((kw, HEAD_DIM), lambda b, c, g: (b * n_chunks + c, 0)),
        ],
        out_specs=pl.BlockSpec((1, hg, 1, HEAD_DIM, kw), lambda b, c, g: (b, g, c, 0, 0)),
        out_shape=jax.ShapeDtypeStruct((batch, n_heads, n_chunks, HEAD_DIM, kw), BF16),
        compiler_params=_params("arbitrary", "arbitrary", "arbitrary"),
        name=name,
    )(x, cos2, sin2)


def _idx_key_kernel(t_ref, g_ref, c_ref, s_ref, o_ref):
    x = t_ref[:, 0:HEAD_DIM]
    y = x * lax.rsqrt(jnp.mean(x * x, axis=-1, keepdims=True) + NORM_EPS) * g_ref[...]
    o_ref[...] = _rope(y, c_ref[...], s_ref[...]).astype(o_ref.dtype)


def _idx_key(tail, g, cos2, sin2, tr=512):
    nt, tw = tail.shape
    tr = min(tr, nt)
    return pl.pallas_call(
        _idx_key_kernel,
        grid=(nt // tr,),
        in_specs=[
            pl.BlockSpec((tr, tw), lambda i: (i, 0)),
            pl.BlockSpec((1, HEAD_DIM), lambda i: (0, 0)),
            pl.BlockSpec((tr, HEAD_DIM), lambda i: (i, 0)),
            pl.BlockSpec((tr, HEAD_DIM), lambda i: (i, 0)),
        ],
        out_specs=pl.BlockSpec((tr, HEAD_DIM), lambda i: (i, 0)),
        out_shape=jax.ShapeDtypeStruct((nt, HEAD_DIM), BF16),
        compiler_params=_params("arbitrary"),
        name="idx_key",
    )(tail, g.reshape(1, HEAD_DIM), cos2, sin2)


def _idx_kernel(qi_ref, kir_ref, tail_ref, c_ref, s_ref, o_ref, qr_ref, wb_ref, sk_ref, run_ref,
                *, n_ih, kw, topk, idx_scale, n_chunks):
    i = pl.program_id(1)
    cq, sq = c_ref[...], s_ref[...]
    wi = tail_ref[:, HEAD_DIM:2 * HEAD_DIM]
    for h in range(n_ih):
        hs = slice(h * BLOCK, (h + 1) * BLOCK)
        qr_ref[hs, :] = _rope(qi_ref[:, hs].astype(F32), cq, sq).astype(BF16)
        wb_ref[h] = jnp.broadcast_to(wi[:, h:h + 1], (BLOCK, HEAD_DIM))
    nc = ((i + 1) * BLOCK + kw - 1) // kw
    n_lt = kw // HEAD_DIM
    qidx = i * BLOCK + lax.broadcasted_iota(jnp.int32, (BLOCK, kw), 0)
    kloc = lax.broadcasted_iota(jnp.int32, (BLOCK, kw), 1)

    def score_chunk(c, _):
        off = pl.multiple_of(c * kw, kw)
        kc = kir_ref[pl.ds(off, kw), :]
        lg = _dot_nt(qr_ref[...], kc)
        accs = [jnp.zeros((BLOCK, HEAD_DIM), F32) for _ in range(n_lt)]
        for h in range(n_ih):
            w = wb_ref[h]
            for t in range(n_lt):
                tile = lg[h * BLOCK:(h + 1) * BLOCK, t * HEAD_DIM:(t + 1) * HEAD_DIM]
                accs[t] = accs[t] + jnp.maximum(tile, 0.0) * w
        acc = jnp.concatenate(accs, axis=1)
        score = jnp.where(c * kw + kloc <= qidx, acc * idx_scale, -jnp.inf)
        bits = lax.bitcast_convert_type(score, jnp.int32)
        bits = jnp.where(bits == INT_MIN, 0, bits)
        sk_ref[c] = jnp.where(bits < 0, bits ^ 0x7FFFFFFF, bits)
        return 0

    lax.fori_loop(0, nc, score_chunk, 0)

    def count_ge(cand):
        cb = jnp.broadcast_to(cand, (BLOCK, HEAD_DIM))

        def body(c, acc):
            x = sk_ref[c]
            for t in range(n_lt):
                acc = acc + (x[:, t * HEAD_DIM:(t + 1) * HEAD_DIM] >= cb).astype(jnp.int32)
            return acc

        acc = lax.fori_loop(0, nc, body, jnp.zeros((BLOCK, HEAD_DIM), jnp.int32))
        return jnp.sum(acc, axis=-1, keepdims=True)

    zero = jnp.zeros((BLOCK, 1), jnp.int32)
    thr = jnp.where(count_ge(zero) >= topk, zero, INT_MIN)

    def bit_step(t, thr):
        cand = thr | (jnp.int32(1) << (30 - t))
        return jnp.where(count_ge(cand) >= topk, cand, thr)

    thr = lax.fori_loop(0, 31, bit_step, thr)
    tb = jnp.broadcast_to(thr, (BLOCK, kw))
    has_ties = jnp.max(count_ge(thr)) > topk

    @pl.when(jnp.logical_not(has_ties))
    def _():
        for c in range(n_chunks):
            @pl.when(c < nc)
            def _():
                sel = (sk_ref[c] >= tb) & (c * kw + kloc <= qidx)
                o_ref[0, c] = jnp.where(sel, 0.0, NEG).astype(o_ref.dtype)

    @pl.when(has_ties)
    def _():
        need = jnp.broadcast_to(topk - count_ge(thr + 1), (BLOCK, kw)).astype(F32)
        rr = lax.broadcasted_iota(jnp.int32, (kw, kw + HEAD_DIM), 0)
        cc = lax.broadcasted_iota(jnp.int32, (kw, kw + HEAD_DIM), 1)
        before = ((rr < cc) | (cc >= kw)).astype(BF16)
        run_ref[...] = jnp.zeros((BLOCK, HEAD_DIM), F32)
        for c in range(n_chunks):
            @pl.when(c < nc)
            def _():
                x = sk_ref[c]
                causal = c * kw + kloc <= qidx
                tied = (x == tb) & causal
                sums = _dot(jnp.where(tied, 1.0, 0.0).astype(BF16), before)
                run = run_ref[...]
                rank = sums[:, :kw] + jnp.concatenate([run] * n_lt, axis=1)
                sel = ((x > tb) & causal) | (tied & (rank < need))
                o_ref[0, c] = jnp.where(sel, 0.0, NEG).astype(o_ref.dtype)
                run_ref[...] = run + sums[:, kw:]

    for c in range(n_chunks):
        @pl.when(c >= nc)
        def _():
            o_ref[0, c] = jnp.full((BLOCK, kw), NEG, o_ref.dtype)


def _indexer(qi, kir, tail, cos2, sin2, batch, seq, n_ih, topk, kw):
    nt = qi.shape[0]
    nq = seq // BLOCK
    n_chunks = seq // kw
    idx_scale = (n_ih ** -0.5) * (HEAD_DIM ** -0.5)
    return pl.pallas_call(
        functools.partial(_idx_kernel, n_ih=n_ih, kw=kw, topk=topk, idx_scale=idx_scale,
                          n_chunks=n_chunks),
        grid=(batch, nq),
        in_specs=[
            pl.BlockSpec((BLOCK, n_ih * HEAD_DIM), lambda b, i: (b * nq + i, 0)),
            pl.BlockSpec((seq, HEAD_DIM), lambda b, i: (b, 0)),
            pl.BlockSpec((BLOCK, 2 * HEAD_DIM), lambda b, i: (b * nq + i, 0)),
            pl.BlockSpec((BLOCK, HEAD_DIM), lambda b, i: (b * nq + i, 0)),
            pl.BlockSpec((BLOCK, HEAD_DIM), lambda b, i: (b * nq + i, 0)),
        ],
        out_specs=pl.BlockSpec((1, n_chunks, BLOCK, kw), lambda b, i: (b * nq + i, 0, 0, 0)),
        out_shape=jax.ShapeDtypeStruct((batch * nq, n_chunks, BLOCK, kw), BF16),
        scratch_shapes=[
            pltpu.VMEM((n_ih * BLOCK, HEAD_DIM), BF16),
            pltpu.VMEM((n_ih, BLOCK, HEAD_DIM), F32),
            pltpu.VMEM((n_chunks, BLOCK, kw), jnp.int32),
            pltpu.VMEM((BLOCK, HEAD_DIM), F32),
        ],
        compiler_params=_params("arbitrary", "arbitrary"),
        name="dsa_indexer",
    )(qi, kir, tail, cos2, sin2)


LOG2E = 1.4426950408889634


def _dsa_kernel(q_ref, kt_ref, v_ref, bias_ref, c_ref, s_ref, o_ref, qs_ref, m_ref, l_ref, acc_ref, sc_ref,
                *, group, qb, kw, scale):
    i = pl.program_id(2)
    n_rg = group * qb
    for g in range(group):
        for t in range(qb):
            r = g * qb + t
            rows = slice(t * BLOCK, (t + 1) * BLOCK)
            qg = _rope(q_ref[rows, g * HEAD_DIM:(g + 1) * HEAD_DIM].astype(F32), c_ref[rows, :], s_ref[rows, :])
            qs_ref[r * BLOCK:(r + 1) * BLOCK, :] = (qg * (scale * LOG2E)).astype(BF16)
            m_ref[r] = jnp.full((BLOCK, HEAD_DIM), NEG, F32)
            l_ref[r] = jnp.zeros((BLOCK, HEAD_DIM), F32)
            acc_ref[r] = jnp.zeros((BLOCK, HEAD_DIM), F32)
    nc = ((i + 1) * qb * BLOCK + kw - 1) // kw
    ones = jnp.ones((kw, HEAD_DIM), BF16)

    def scores(c):
        return _dot(qs_ref[...], kt_ref[0, 0, c])

    def half(c, src, dst):
        s_all = sc_ref[src]
        sc_ref[dst] = scores(jnp.minimum(c + 1, kt_ref.shape[2] - 1))
        off = pl.multiple_of(c * kw, kw)
        vx = jnp.concatenate([v_ref[pl.ds(off, kw), :], ones], axis=1)
        bs = [bias_ref[t, c].astype(F32) for t in range(qb)]
        ps, alphas = [], []
        for r in range(n_rg):
            s = s_all[r * BLOCK:(r + 1) * BLOCK] + bs[r % qb]
            m_old = m_ref[r]
            m_new = jnp.maximum(m_old, jnp.max(s, axis=-1, keepdims=True))
            alphas.append(jnp.exp2(m_old - m_new))
            ps.append(jnp.exp2(s - jnp.concatenate([m_new] * (kw // HEAD_DIM), axis=1)).astype(BF16))
            m_ref[r] = m_new
        pv = _dot(jnp.concatenate(ps, axis=0), vx)
        for r in range(n_rg):
            pr = pv[r * BLOCK:(r + 1) * BLOCK]
            acc_ref[r] = acc_ref[r] * alphas[r] + pr[:, :HEAD_DIM]
            l_ref[r] = l_ref[r] * alphas[r] + pr[:, HEAD_DIM:]

    def body(t, _):
        half(2 * t, 0, 1)
        half(2 * t + 1, 1, 0)
        return 0

    sc_ref[0] = scores(0)
    lax.fori_loop(0, (nc + 1) // 2, body, 0)
    for g in range(group):
        for t in range(qb):
            r = g * qb + t
            o_ref[t * BLOCK:(t + 1) * BLOCK, g * HEAD_DIM:(g + 1) * HEAD_DIM] = (
                acc_ref[r] / l_ref[r]).astype(o_ref.dtype)


DSA_Q_BLOCKS = 1


def _dsa(qkv, kr, bias, cos2, sin2, batch, seq, n_heads, n_kv, kw):
    nt = qkv.shape[0]
    qb = DSA_Q_BLOCKS if (seq // BLOCK) % DSA_Q_BLOCKS == 0 else 1
    tq = qb * BLOCK
    nq = seq // tq
    group = n_heads // n_kv
    n_rg = group * qb
    n_chunks = seq // kw
    assert n_chunks % 2 == 0
    vcol = n_heads + n_kv
    return pl.pallas_call(
        functools.partial(_dsa_kernel, group=group, qb=qb, kw=kw, scale=HEAD_DIM ** -0.5),
        grid=(batch, n_kv, nq),
        in_specs=[
            pl.BlockSpec((tq, group * HEAD_DIM), lambda b, h, i: (b * nq + i, h)),
            pl.BlockSpec((1, 1, n_chunks, HEAD_DIM, kw), lambda b, h, i: (b, h, 0, 0, 0)),
            pl.BlockSpec((seq, HEAD_DIM), lambda b, h, i: (b, vcol + h)),
            pl.BlockSpec((qb, n_chunks, BLOCK, kw), lambda b, h, i: (b * nq + i, 0, 0, 0)),
            pl.BlockSpec((tq, HEAD_DIM), lambda b, h, i: (b * nq + i, 0)),
            pl.BlockSpec((tq, HEAD_DIM), lambda b, h, i: (b * nq + i, 0)),
        ],
        out_specs=pl.BlockSpec((tq, group * HEAD_DIM), lambda b, h, i: (b * nq + i, h)),
        out_shape=jax.ShapeDtypeStruct((nt, n_heads * HEAD_DIM), BF16),
        scratch_shapes=[
            pltpu.VMEM((n_rg * BLOCK, HEAD_DIM), BF16),
            pltpu.VMEM((n_rg, BLOCK, HEAD_DIM), F32),
            pltpu.VMEM((n_rg, BLOCK, HEAD_DIM), F32),
            pltpu.VMEM((n_rg, BLOCK, HEAD_DIM), F32),
            pltpu.VMEM((2, n_rg * BLOCK, kw), F32),
        ],
        compiler_params=_params("arbitrary", "arbitrary", "arbitrary"),
        name="dsa_attn",
    )(qkv, kr, qkv, bias, cos2, sin2)


def _rope_tables(positions):
    inv_freq = ROPE_THETA ** (-jnp.arange(0, HEAD_DIM, 2, dtype=F32) / HEAD_DIM)
    ang = positions.astype(F32).reshape(-1)[:, None] * inv_freq
    cos, sin = jnp.cos(ang), jnp.sin(ang)
    return jnp.concatenate([cos, cos], axis=-1), jnp.concatenate([-sin, sin], axis=-1)


def _pick_tile(n, prefs):
    for t in prefs:
        if n % t == 0:
            return t
    return n


def kernel(x, c, positions, norm1_g, norm2_g, ada_w, ada_b, even_w_in, even_sinks, even_w_o,
           odd_w_in, odd_idx_k_g, odd_w_o, ffn_w_gate, ffn_w_up, ffn_conv_w, ffn_conv_b,
           ffn_w_down, final_g):
    batch, seq, d = x.shape
    depth = norm1_g.shape[0]
    nt = batch * seq
    a_heads = even_sinks.shape[1]
    a_kv = max(1, a_heads // 8)
    b_heads = even_w_o.shape[1] // HEAD_DIM - a_heads
    c_heads = odd_w_o.shape[1] // HEAD_DIM
    c_kv = c_heads // 4
    odd_in = odd_w_in.shape[2]
    n_ih = (odd_in - (c_heads + 2 * c_kv) * HEAD_DIM - HEAD_DIM) // (HEAD_DIM + 1)
    topk = min(TOPK_MAX, seq // 4)
    kw = min(512, seq)

    cos2, sin2 = _rope_tables(positions)
    c_pad = jnp.zeros((8, d), F32).at[:batch].set(c)
    mods = _ada(c_pad, ada_w, ada_b, tn=_pick_tile(6 * d, (512, 256, 128)))[:, :batch]
    mods = mods.reshape(depth, batch, 6, 1, d)

    tm = 1024
    w_down = ffn_w_down.astype(BF16)
    h = x.reshape(nt, d)
    for layer in range(depth):
        shift1, scale1, gate1, shift2, scale2, gate2 = [mods[layer, :, k] for k in range(6)]
        j = layer // 2
        hn = _norm_mod(h, norm1_g[layer], scale1, shift1, seq)
        if layer % 2 == 0:
            w_in = even_w_in
            n_in = w_in.shape[2]
            proj = _matmul([hn], w_in, n_in, 0, tm, _pick_tile(n_in, (512, 256, 128)), BF16, "in_proj_even",
                           layer=j)
            oa = _swa(proj, cos2, sin2, even_sinks[j], batch, seq, a_heads, a_kv)
            qcol = a_heads + 2 * a_kv
            kt = _heads_t(proj, qcol + b_heads, b_heads, cos2, sin2, batch, seq, min(SB_TILE, seq), False,
                          "sb_kt")
            ob = _sb(proj, kt, batch, seq, b_heads, qcol, qcol + 2 * b_heads)
            w_o = even_w_o
            if a_heads == b_heads:
                parts = [oa, ob]
            else:
                parts = [jnp.concatenate([oa, ob], axis=-1)]
            h = _matmul(parts, w_o, d, 0, tm, _pick_tile(d, (512, 256, 128)), F32, "out_proj_even",
                        res=h, gate=gate1, seq=seq, layer=j)
        else:
            w_in = odd_w_in.astype(BF16)
            n_qkv = (c_heads + 2 * c_kv) * HEAD_DIM
            n_qi = n_ih * HEAD_DIM
            qkv = _matmul([hn], w_in, n_qkv, 0, tm, _pick_tile(n_qkv, (512, 256, 128)), BF16, "in_proj_qkv",
                          layer=j)
            tn_qi = _pick_tile(n_qi, [t for t in (512, 256, 128) if n_qkv % t == 0])
            qi = _matmul([hn], w_in, n_qi, n_qkv, tm, tn_qi, BF16, "in_proj_qi", layer=j)
            w_tail = jnp.zeros((1, d, 2 * HEAD_DIM), BF16).at[0, :, :HEAD_DIM + n_ih].set(
                w_in[j, :, n_qkv + n_qi:].astype(BF16))
            tail = _matmul([hn], w_tail, 2 * HEAD_DIM, 0, tm, 2 * HEAD_DIM, F32, "in_proj_idx")
            kr = _heads_t(qkv, c_heads, c_kv, cos2, sin2, batch, seq, kw, True, "rope_k")
            kir = _idx_key(tail, odd_idx_k_g[j], cos2, sin2)
            bias = _indexer(qi, kir, tail, cos2, sin2, batch, seq, n_ih, topk, kw)
            o = _dsa(qkv, kr, bias, cos2, sin2, batch, seq, c_heads, c_kv, kw)
            w_o = odd_w_o
            h = _matmul([o], w_o, d, 0, tm, _pick_tile(d, (512, 256, 128)), F32, "out_proj_odd",
                        res=h, gate=gate1, seq=seq, layer=j)
        hn = _norm_mod(h, norm2_g[layer], scale2, shift2, seq)
        a = _ffn_up(hn, ffn_w_gate, ffn_w_up, layer, ffn_conv_w[layer], ffn_conv_b[layer], seq)
        h = _matmul([a], w_down, d, 0, 1024, 256, F32, "ffn_down", res=h, gate=gate2, seq=seq, layer=layer,
                    single_buffer_x=True)
    return _norm_plain(h, final_g).reshape(batch, seq, d)
```

```python
import functools
import math

import jax
import jax.numpy as jnp
from jax import lax
from jax.experimental import pallas as pl
from jax.experimental.pallas import tpu as pltpu

F32 = jnp.float32
BF16 = jnp.bfloat16

HEAD_DIM = 128
BLOCK = 128
ROPE_THETA = 10000.0
NORM_EPS = 1e-6
TOPK_MAX = 256
NEG = -1e30
INT_MIN = -(2 ** 31)
VMEM_LIMIT_BYTES = 56 * 1024 * 1024


def _params(*sem):
    return pltpu.CompilerParams(dimension_semantics=sem, vmem_limit_bytes=VMEM_LIMIT_BYTES)


def _dot(a, b):
    return jnp.dot(a, b, preferred_element_type=F32)


def _dot_nt(a, b):
    return lax.dot_general(a, b, (((1,), (1,)), ((), ())), preferred_element_type=F32)


def _rope(x, cos2, sin2):
    return x * cos2 + pltpu.roll(x, HEAD_DIM // 2, axis=1) * sin2


def _ada_kernel(c_ref, w_ref, b_ref, o_ref):
    c = c_ref[...]
    ca = (c * jax.nn.sigmoid(c)).astype(BF16)
    o_ref[0] = _dot(ca, w_ref[0].astype(BF16)) + b_ref[0]


def _ada(c_pad, ada_w, ada_b, tn=512):
    n_layers, d, n = ada_w.shape
    rows = c_pad.shape[0]
    return pl.pallas_call(
        _ada_kernel,
        grid=(n_layers, n // tn),
        in_specs=[
            pl.BlockSpec((rows, d), lambda l, j: (0, 0)),
            pl.BlockSpec((1, d, tn), lambda l, j: (l, 0, j)),
            pl.BlockSpec((1, 1, tn), lambda l, j: (l, 0, j)),
        ],
        out_specs=pl.BlockSpec((1, rows, tn), lambda l, j: (l, 0, j)),
        out_shape=jax.ShapeDtypeStruct((n_layers, rows, n), F32),
        compiler_params=_params("arbitrary", "arbitrary"),
        name="ada_mod",
    )(c_pad, ada_w, ada_b.reshape(n_layers, 1, n))


def _norm_mod_kernel(x_ref, g_ref, sc_ref, sh_ref, o_ref):
    x = x_ref[...]
    y = x * lax.rsqrt(jnp.mean(x * x, axis=-1, keepdims=True) + NORM_EPS) * g_ref[...]
    o_ref[...] = (y * (1.0 + sc_ref[0]) + sh_ref[0]).astype(o_ref.dtype)


def _norm_plain_kernel(x_ref, g_ref, o_ref):
    x = x_ref[...]
    y = x * lax.rsqrt(jnp.mean(x * x, axis=-1, keepdims=True) + NORM_EPS) * g_ref[...]
    o_ref[...] = y.astype(o_ref.dtype)


def _norm_mod(h, g, scale, shift, seq, tr=512):
    nt, d = h.shape
    tr = min(tr, seq)
    bps = seq // tr
    return pl.pallas_call(
        _norm_mod_kernel,
        grid=(nt // tr,),
        in_specs=[
            pl.BlockSpec((tr, d), lambda i: (i, 0)),
            pl.BlockSpec((1, d), lambda i: (0, 0)),
            pl.BlockSpec((1, 1, d), lambda i: (i // bps, 0, 0)),
            pl.BlockSpec((1, 1, d), lambda i: (i // bps, 0, 0)),
        ],
        out_specs=pl.BlockSpec((tr, d), lambda i: (i, 0)),
        out_shape=jax.ShapeDtypeStruct((nt, d), BF16),
        compiler_params=_params("arbitrary"),
        name="norm_mod",
    )(h, g.reshape(1, d), scale, shift)


def _norm_plain(h, g, tr=512):
    nt, d = h.shape
    tr = min(tr, nt)
    return pl.pallas_call(
        _norm_plain_kernel,
        grid=(nt // tr,),
        in_specs=[pl.BlockSpec((tr, d), lambda i: (i, 0)), pl.BlockSpec((1, d), lambda i: (0, 0))],
        out_specs=pl.BlockSpec((tr, d), lambda i: (i, 0)),
        out_shape=jax.ShapeDtypeStruct((nt, d), F32),
        compiler_params=_params("arbitrary"),
        name="norm_final",
    )(h, g.reshape(1, d))


def _mm_kernel(*refs, n_parts, has_res):
    acc = None
    for x_ref, w_ref in zip(refs[:n_parts], refs[n_parts:2 * n_parts]):
        d = _dot(x_ref[...], w_ref[...].astype(BF16))
        acc = d if acc is None else acc + d
    rest = refs[2 * n_parts:]
    if has_res:
        res_ref, gate_ref, o_ref = rest
        o_ref[...] = res_ref[...] + gate_ref[0] * acc
    else:
        (o_ref,) = rest
        o_ref[...] = acc.astype(o_ref.dtype)


def _matmul(xs, w, n_out, col_off, tm, tn, out_dtype, name, res=None, gate=None, seq=None, layer=0,
            single_buffer_x=False):
    m, kp = xs[0].shape
    tm = min(tm, m if seq is None else seq)
    tn = min(tn, n_out)
    assert m % tm == 0 and n_out % tn == 0 and col_off % tn == 0
    cb = col_off // tn
    n_parts = len(xs)
    x_mode = dict(pipeline_mode=pl.Buffered(1)) if single_buffer_x else {}
    in_specs = [pl.BlockSpec((tm, kp), lambda i, j: (i, 0), **x_mode) for _ in xs]
    in_specs += [pl.BlockSpec((None, kp, tn), lambda i, j, p=p: (layer, p, j + cb)) for p in range(n_parts)]
    args = list(xs) + [w] * n_parts
    if res is not None:
        bps = seq // tm
        in_specs += [pl.BlockSpec((tm, tn), lambda i, j: (i, j)),
                     pl.BlockSpec((1, 1, tn), lambda i, j: (i // bps, 0, j))]
        args += [res, gate]
    return pl.pallas_call(
        functools.partial(_mm_kernel, n_parts=n_parts, has_res=res is not None),
        grid=(m // tm, n_out // tn),
        in_specs=in_specs,
        out_specs=pl.BlockSpec((tm, tn), lambda i, j: (i, j)),
        out_shape=jax.ShapeDtypeStruct((m, n_out), out_dtype),
        compiler_params=_params("arbitrary", "arbitrary"),
        name=name,
    )(*args)


HALO = 16


def _ffn_up_kernel(x_ref, xh_ref, wg_ref, wu_ref, cw_ref, cb_ref, o_ref, gs_ref, *, tm, bps):
    i = pl.program_id(0)
    x = x_ref[...]
    wg = wg_ref[...].astype(BF16)
    g = _dot(x, wg)
    u = _dot(x, wu_ref[...].astype(BF16))
    gs_ref[0:HALO, :] = jnp.where(i % bps == 0, 0.0, _dot(xh_ref[...], wg))
    gs_ref[HALO:, :] = g
    g1 = gs_ref[HALO - 1:HALO - 1 + tm, :]
    g2 = gs_ref[HALO - 2:HALO - 2 + tm, :]
    cw = cw_ref[...]
    gc = cw[0:1] * g2 + cw[1:2] * g1 + cw[2:3] * g + cb_ref[...]
    o_ref[...] = (gc * jax.nn.sigmoid(gc) * u).astype(o_ref.dtype)


def _ffn_up(hn, wg, wu, layer, conv_w, conv_b, seq, tm=1024, tn=256):
    m, d = hn.shape
    dff = wg.shape[2]
    tm = min(tm, seq)
    assert dff % tn == 0 and tm % HALO == 0
    bps = seq // tm
    hpb = tm // HALO
    return pl.pallas_call(
        functools.partial(_ffn_up_kernel, tm=tm, bps=bps),
        grid=(m // tm, dff // tn),
        in_specs=[
            pl.BlockSpec((tm, d), lambda i, j: (i, 0)),
            pl.BlockSpec((HALO, d), lambda i, j: (jnp.maximum(i * hpb - 1, 0), 0)),
            pl.BlockSpec((None, d, tn), lambda i, j: (layer, 0, j)),
            pl.BlockSpec((None, d, tn), lambda i, j: (layer, 0, j)),
            pl.BlockSpec((3, tn), lambda i, j: (0, j)),
            pl.BlockSpec((1, tn), lambda i, j: (0, j)),
        ],
        out_specs=pl.BlockSpec((tm, tn), lambda i, j: (i, j)),
        out_shape=jax.ShapeDtypeStruct((m, dff), BF16),
        scratch_shapes=[pltpu.VMEM((tm + HALO, tn), F32)],
        compiler_params=_params("arbitrary", "arbitrary"),
        name="ffn_up",
    )(hn, hn, wg, wu, conv_w, conv_b.reshape(1, dff))


def _swa_kernel(sink_ref, q_ref, kc_ref, kp_ref, vc_ref, vp_ref, cq_ref, sq_ref, cp_ref, sp_ref,
                o_ref, *, n_heads, n_kv, scale):
    i = pl.program_id(1)
    cq, sq, cp, sp = cq_ref[...], sq_ref[...], cp_ref[...], sp_ref[...]
    group = n_heads // n_kv
    qi = lax.broadcasted_iota(jnp.int32, (BLOCK, BLOCK), 0)
    kj = lax.broadcasted_iota(jnp.int32, (BLOCK, BLOCK), 1)
    mask = jnp.concatenate([(kj > qi) & (i > 0), kj <= qi], axis=1)
    for hk in range(n_kv):
        sl = slice(hk * HEAD_DIM, (hk + 1) * HEAD_DIM)
        k2 = jnp.concatenate([_rope(kp_ref[:, sl].astype(F32), cp, sp),
                              _rope(kc_ref[:, sl].astype(F32), cq, sq)], axis=0)
        k2t = k2.T.astype(BF16)
        v2 = jnp.concatenate([vp_ref[:, sl], vc_ref[:, sl]], axis=0)
        heads = [hk * group + g for g in range(group)]
        qs = jnp.concatenate(
            [_rope(q_ref[:, h * HEAD_DIM:(h + 1) * HEAD_DIM].astype(F32), cq, sq).astype(BF16)
             for h in heads], axis=0)
        s_all = _dot(qs, k2t) * scale
        ps, dens = [], []
        for g, h in enumerate(heads):
            s = jnp.where(mask, s_all[g * BLOCK:(g + 1) * BLOCK], NEG)
            sink = sink_ref[h]
            m = jnp.maximum(jnp.max(s, axis=-1, keepdims=True), sink)
            p = jnp.exp(s - m)
            dens.append(jnp.sum(p, axis=-1, keepdims=True) + jnp.exp(sink - m))
            ps.append(p.astype(BF16))
        o_all = _dot(jnp.concatenate(ps, axis=0), v2)
        for g, h in enumerate(heads):
            o_ref[:, h * HEAD_DIM:(h + 1) * HEAD_DIM] = (
                o_all[g * BLOCK:(g + 1) * BLOCK] / dens[g]).astype(o_ref.dtype)


def _swa(proj, cos2, sin2, sinks, batch, seq, n_heads, n_kv):
    nt = proj.shape[0]
    nq = seq // BLOCK
    qw, kw = n_heads * HEAD_DIM, n_kv * HEAD_DIM
    kcol = n_heads // n_kv
    vcol = kcol + 1
    cur = lambda b, i: b * nq + i
    prev = lambda b, i: b * nq + jnp.maximum(i - 1, 0)
    return pl.pallas_call(
        functools.partial(_swa_kernel, n_heads=n_heads, n_kv=n_kv, scale=HEAD_DIM ** -0.5),
        grid=(batch, nq),
        in_specs=[
            pl.BlockSpec(memory_space=pltpu.SMEM),
            pl.BlockSpec((BLOCK, qw), lambda b, i: (cur(b, i), 0)),
            pl.BlockSpec((BLOCK, kw), lambda b, i: (cur(b, i), kcol)),
            pl.BlockSpec((BLOCK, kw), lambda b, i: (prev(b, i), kcol)),
            pl.BlockSpec((BLOCK, kw), lambda b, i: (cur(b, i), vcol)),
            pl.BlockSpec((BLOCK, kw), lambda b, i: (prev(b, i), vcol)),
            pl.BlockSpec((BLOCK, HEAD_DIM), lambda b, i: (cur(b, i), 0)),
            pl.BlockSpec((BLOCK, HEAD_DIM), lambda b, i: (cur(b, i), 0)),
            pl.BlockSpec((BLOCK, HEAD_DIM), lambda b, i: (prev(b, i), 0)),
            pl.BlockSpec((BLOCK, HEAD_DIM), lambda b, i: (prev(b, i), 0)),
        ],
        out_specs=pl.BlockSpec((BLOCK, qw), lambda b, i: (cur(b, i), 0)),
        out_shape=jax.ShapeDtypeStruct((nt, qw), BF16),
        compiler_params=_params("arbitrary", "arbitrary"),
        name="swa_attn",
    )(sinks, proj, proj, proj, proj, proj, cos2, sin2, cos2, sin2)


SB_TILE = 256
SB_HEADS_PER_STEP = 4
SB_DROP_CAP = 104.0


def _sb_kernel(q_ref, kt_ref, v_ref, o_ref, acc_ref, run_ref, *, scale, tile, hp):
    i = pl.program_id(2)
    row = lax.broadcasted_iota(jnp.int32, (tile, tile), 0)
    col = lax.broadcasted_iota(jnp.int32, (tile, tile), 1)
    strict = col < row
    rr = lax.broadcasted_iota(jnp.int32, (tile, tile + HEAD_DIM), 0)
    cc = lax.broadcasted_iota(jnp.int32, (tile, tile + HEAD_DIM), 1)
    cum = ((rr > cc) | (cc >= tile)).astype(BF16)

    def head_step(h, j, diag):
        hs = slice(h * HEAD_DIM, (h + 1) * HEAD_DIM)
        off = pl.multiple_of(j * tile, tile)
        v = v_ref[pl.ds(off, tile), hs]
        z = _dot(q_ref[:, hs], kt_ref[0, h, j]) * scale
        soft = jnp.log(1.0 + jnp.exp(-jnp.abs(z)))
        log_beta = jnp.minimum(z, 0.0) - soft
        drop = jnp.maximum(z, 0.0) + soft
        if diag:
            drop = jnp.where(strict, drop, 0.0)
        hi = drop.astype(BF16)
        lo = (drop - hi.astype(F32)).astype(BF16)
        sums = _dot(hi, cum) + _dot(lo, cum)
        after = sums[:, :tile]
        if not diag:
            run = run_ref[h]
            after = after + jnp.concatenate([run] * (tile // HEAD_DIM), axis=1)
        w = jnp.exp(log_beta - after)
        if diag:
            w = jnp.where(strict, w, 0.0)
        pv = _dot(w.astype(BF16), v)
        if diag:
            acc_ref[h] = pv
            run = sums[:, tile:]
        else:
            acc_ref[h] += pv
            run = run + sums[:, tile:]
        run_ref[h] = run
        return jnp.min(run)

    def step(j, diag):
        least = head_step(0, j, diag)
        for h in range(1, hp):
            least = jnp.minimum(least, head_step(h, j, diag))
        return least

    def cond(st):
        return (st[0] >= 0) & (st[1] < SB_DROP_CAP)

    def body(st):
        return st[0] - 1, step(st[0], False)

    lax.while_loop(cond, body, (i - 1, step(i, True)))
    for h in range(hp):
        o_ref[:, h * HEAD_DIM:(h + 1) * HEAD_DIM] = acc_ref[h].astype(o_ref.dtype)


def _sb(proj, kt, batch, seq, n_heads, qcol, vcol):
    nt = proj.shape[0]
    tile = min(SB_TILE, seq)
    nq = seq // tile
    hp = math.gcd(math.gcd(qcol, vcol), math.gcd(n_heads, SB_HEADS_PER_STEP))
    w = hp * HEAD_DIM
    return pl.pallas_call(
        functools.partial(_sb_kernel, scale=HEAD_DIM ** -0.5, tile=tile, hp=hp),
        grid=(batch, n_heads // hp, nq),
        in_specs=[
            pl.BlockSpec((tile, w), lambda b, h, i: (b * nq + i, qcol // hp + h)),
            pl.BlockSpec((1, hp, nq, HEAD_DIM, tile), lambda b, h, i: (b, h, 0, 0, 0)),
            pl.BlockSpec((seq, w), lambda b, h, i: (b, vcol // hp + h)),
        ],
        out_specs=pl.BlockSpec((tile, w), lambda b, h, i: (b * nq + i, h)),
        out_shape=jax.ShapeDtypeStruct((nt, n_heads * HEAD_DIM), BF16),
        scratch_shapes=[pltpu.VMEM((hp, tile, HEAD_DIM), F32), pltpu.VMEM((hp, tile, HEAD_DIM), F32)],
        compiler_params=_params("arbitrary", "arbitrary", "arbitrary"),
        name="sb_attn",
    )(proj, kt, proj)


def _rope_heads_t_kernel(x_ref, c_ref, s_ref, o_ref, *, n_heads, rotate, kw, cps):
    for j in range(cps):
        rows = slice(j * kw, (j + 1) * kw)
        c, s = c_ref[rows, :], s_ref[rows, :]
        for h in range(n_heads):
            x = x_ref[rows, h * HEAD_DIM:(h + 1) * HEAD_DIM].astype(F32)
            if rotate:
                x = _rope(x, c, s)
            o_ref[0, h, j] = x.T.astype(o_ref.dtype)


def _heads_t(x, first_head, n_heads, cos2, sin2, batch, seq, kw, rotate, name):
    hg = math.gcd(first_head, n_heads)
    n_chunks = seq // kw
    cps = math.gcd(n_chunks, max(1, 1024 // kw))
    steps = n_chunks // cps
    return pl.pallas_call(
        functools.partial(_rope_heads_t_kernel, n_heads=hg, rotate=rotate, kw=kw, cps=cps),
        grid=(batch, steps, n_heads // hg),
        in_specs=[
            pl.BlockSpec((cps * kw, hg * HEAD_DIM), lambda b, c, g: (b * steps + c, first_head // hg + g)),
            pl.BlockSpec((cps * kw, HEAD_DIM), lambda b, c, g: (b * steps + c, 0)),
            pl.BlockSpec((cps * kw, HEAD_DIM), lambda b, c, g: (b * steps + c, 0)),
        ],
        out_specs=pl.BlockSpec((1, hg, cps, HEAD_DIM, kw), lambda b, c, g: (b, g, c, 0, 0)),
        out_shape=jax.ShapeDtypeStruct((batch, n_heads, n_chunks, HEAD_DIM, kw), BF16),
        compiler_params=_params("arbitrary", "arbitrary", "arbitrary"),
        name=name,
    )(x, cos2, sin2)


def _idx_key_kernel(t_ref, g_ref, c_ref, s_ref, o_ref):
    x = t_ref[:, 0:HEAD_DIM]
    y = x * lax.rsqrt(jnp.mean(x * x, axis=-1, keepdims=True) + NORM_EPS) * g_ref[...]
    o_ref[...] = _rope(y, c_ref[...], s_ref[...]).astype(o_ref.dtype)


def _idx_key(tail, g, cos2, sin2, tr=512):
    nt, tw = tail.shape
    tr = min(tr, nt)
    return pl.pallas_call(
        _idx_key_kernel,
        grid=(nt // tr,),
        in_specs=[
            pl.BlockSpec((tr, tw), lambda i: (i, 0)),
            pl.BlockSpec((1, HEAD_DIM), lambda i: (0, 0)),
            pl.BlockSpec((tr, HEAD_DIM), lambda i: (i, 0)),
            pl.BlockSpec((tr, HEAD_DIM), lambda i: (i, 0)),
        ],
        out_specs=pl.BlockSpec((tr, HEAD_DIM), lambda i: (i, 0)),
        out_shape=jax.ShapeDtypeStruct((nt, HEAD_DIM), BF16),
        compiler_params=_params("arbitrary"),
        name="idx_key",
    )(tail, g.reshape(1, HEAD_DIM), cos2, sin2)


def _idx_kernel(qi_ref, kir_ref, tail_ref, c_ref, s_ref, o_ref, qr_ref, wb_ref, sk_ref, run_ref,
                *, n_ih, kw, topk, idx_scale, n_chunks):
    i = pl.program_id(1)
    cq, sq = c_ref[...], s_ref[...]
    wi = tail_ref[:, HEAD_DIM:2 * HEAD_DIM]
    for h in range(n_ih):
        hs = slice(h * BLOCK, (h + 1) * BLOCK)
        qr_ref[hs, :] = _rope(qi_ref[:, hs].astype(F32), cq, sq).astype(BF16)
        wb_ref[h] = jnp.broadcast_to(wi[:, h:h + 1], (BLOCK, HEAD_DIM))
    nc = ((i + 1) * BLOCK + kw - 1) // kw
    n_lt = kw // HEAD_DIM
    qidx = i * BLOCK + lax.broadcasted_iota(jnp.int32, (BLOCK, kw), 0)
    kloc = lax.broadcasted_iota(jnp.int32, (BLOCK, kw), 1)

    def score_chunk(c, _):
        off = pl.multiple_of(c * kw, kw)
        kc = kir_ref[pl.ds(off, kw), :]
        lg = _dot_nt(qr_ref[...], kc)
        accs = [jnp.zeros((BLOCK, HEAD_DIM), F32) for _ in range(n_lt)]
        for h in range(n_ih):
            w = wb_ref[h]
            for t in range(n_lt):
                tile = lg[h * BLOCK:(h + 1) * BLOCK, t * HEAD_DIM:(t + 1) * HEAD_DIM]
                accs[t] = accs[t] + jnp.maximum(tile, 0.0) * w
        acc = jnp.concatenate(accs, axis=1)
        score = jnp.where(c * kw + kloc <= qidx, acc * idx_scale, -jnp.inf)
        bits = lax.bitcast_convert_type(score, jnp.int32)
        bits = jnp.where(bits == INT_MIN, 0, bits)
        sk_ref[c] = jnp.where(bits < 0, bits ^ 0x7FFFFFFF, bits)
        return 0

    lax.fori_loop(0, nc, score_chunk, 0)

    def count_ge(cand):
        cb = jnp.broadcast_to(cand, (BLOCK, HEAD_DIM))

        def body(c, acc):
            x = sk_ref[c]
            for t in range(n_lt):
                acc = acc + jnp.where(x[:, t * HEAD_DIM:(t + 1) * HEAD_DIM] >= cb, 1.0, 0.0)
            return acc

        acc = lax.fori_loop(0, nc, body, jnp.zeros((BLOCK, HEAD_DIM), F32))
        return jnp.sum(acc, axis=-1, keepdims=True)

    zero = jnp.zeros((BLOCK, 1), jnp.int32)
    thr = jnp.where(count_ge(zero) >= topk, zero, INT_MIN)

    def bit_step(t, thr):
        cand = thr | (jnp.int32(1) << (30 - t))
        return jnp.where(count_ge(cand) >= topk, cand, thr)

    thr = lax.fori_loop(0, 31, bit_step, thr)
    tb = jnp.broadcast_to(thr, (BLOCK, kw))
    has_ties = jnp.max(count_ge(thr)) > topk

    @pl.when(jnp.logical_not(has_ties))
    def _():
        for c in range(n_chunks):
            @pl.when(c < nc)
            def _():
                sel = (sk_ref[c] >= tb) & (c * kw + kloc <= qidx)
                o_ref[0, c] = jnp.where(sel, 0.0, NEG).astype(o_ref.dtype)

    @pl.when(has_ties)
    def _():
        need = jnp.broadcast_to(topk - count_ge(thr + 1), (BLOCK, kw)).astype(F32)
        rr = lax.broadcasted_iota(jnp.int32, (kw, kw + HEAD_DIM), 0)
        cc = lax.broadcasted_iota(jnp.int32, (kw, kw + HEAD_DIM), 1)
        before = ((rr < cc) | (cc >= kw)).astype(BF16)
        run_ref[...] = jnp.zeros((BLOCK, HEAD_DIM), F32)
        for c in range(n_chunks):
            @pl.when(c < nc)
            def _():
                x = sk_ref[c]
                causal = c * kw + kloc <= qidx
                tied = (x == tb) & causal
                sums = _dot(jnp.where(tied, 1.0, 0.0).astype(BF16), before)
                run = run_ref[...]
                rank = sums[:, :kw] + jnp.concatenate([run] * n_lt, axis=1)
                sel = ((x > tb) & causal) | (tied & (rank < need))
                o_ref[0, c] = jnp.where(sel, 0.0, NEG).astype(o_ref.dtype)
                run_ref[...] = run + sums[:, kw:]

    for c in range(n_chunks):
        @pl.when(c >= nc)
        def _():
            o_ref[0, c] = jnp.full((BLOCK, kw), NEG, o_ref.dtype)


def _indexer(qi, kir, tail, cos2, sin2, batch, seq, n_ih, topk, kw):
    nt = qi.shape[0]
    nq = seq // BLOCK
    n_chunks = seq // kw
    idx_scale = (n_ih ** -0.5) * (HEAD_DIM ** -0.5)
    return pl.pallas_call(
        functools.partial(_idx_kernel, n_ih=n_ih, kw=kw, topk=topk, idx_scale=idx_scale,
                          n_chunks=n_chunks),
        grid=(batch, nq),
        in_specs=[
            pl.BlockSpec((BLOCK, n_ih * HEAD_DIM), lambda b, i: (b * nq + i, 0)),
            pl.BlockSpec((seq, HEAD_DIM), lambda b, i: (b, 0)),
            pl.BlockSpec((BLOCK, 2 * HEAD_DIM), lambda b, i: (b * nq + i, 0)),
            pl.BlockSpec((BLOCK, HEAD_DIM), lambda b, i: (b * nq + i, 0)),
            pl.BlockSpec((BLOCK, HEAD_DIM), lambda b, i: (b * nq + i, 0)),
        ],
        out_specs=pl.BlockSpec((1, n_chunks, BLOCK, kw), lambda b, i: (b * nq + i, 0, 0, 0)),
        out_shape=jax.ShapeDtypeStruct((batch * nq, n_chunks, BLOCK, kw), BF16),
        scratch_shapes=[
            pltpu.VMEM((n_ih * BLOCK, HEAD_DIM), BF16),
            pltpu.VMEM((n_ih, BLOCK, HEAD_DIM), F32),
            pltpu.VMEM((n_chunks, BLOCK, kw), jnp.int32),
            pltpu.VMEM((BLOCK, HEAD_DIM), F32),
        ],
        compiler_params=_params("arbitrary", "arbitrary"),
        name="dsa_indexer",
    )(qi, kir, tail, cos2, sin2)


LOG2E = 1.4426950408889634


def _dsa_kernel(q_ref, kt_ref, v_ref, bias_ref, c_ref, s_ref, o_ref, qs_ref, m_ref, l_ref, acc_ref, sc_ref,
                *, group, qb, kw, scale):
    i = pl.program_id(2)
    n_rg = group * qb
    for g in range(group):
        for t in range(qb):
            r = g * qb + t
            rows = slice(t * BLOCK, (t + 1) * BLOCK)
            qg = _rope(q_ref[rows, g * HEAD_DIM:(g + 1) * HEAD_DIM].astype(F32), c_ref[rows, :], s_ref[rows, :])
            qs_ref[r * BLOCK:(r + 1) * BLOCK, :] = (qg * (scale * LOG2E)).astype(BF16)
            m_ref[r] = jnp.full((BLOCK, HEAD_DIM), NEG, F32)
            l_ref[r] = jnp.zeros((BLOCK, HEAD_DIM), F32)
            acc_ref[r] = jnp.zeros((BLOCK, HEAD_DIM), F32)
    nc = ((i + 1) * qb * BLOCK + kw - 1) // kw
    ones = jnp.ones((kw, HEAD_DIM), BF16)

    def scores(c):
        return _dot(qs_ref[...], kt_ref[0, 0, c])

    def half(c, src, dst):
        s_all = sc_ref[src]
        sc_ref[dst] = scores(jnp.minimum(c + 1, kt_ref.shape[2] - 1))
        off = pl.multiple_of(c * kw, kw)
        vx = jnp.concatenate([v_ref[pl.ds(off, kw), :], ones], axis=1)
        bs = [bias_ref[t, c].astype(F32) for t in range(qb)]
        ps, alphas = [], []
        for r in range(n_rg):
            s = s_all[r * BLOCK:(r + 1) * BLOCK] + bs[r % qb]
            m_old = m_ref[r]
            m_new = jnp.maximum(m_old, jnp.max(s, axis=-1, keepdims=True))
            alphas.append(jnp.exp2(m_old - m_new))
            ps.append(jnp.exp2(s - jnp.concatenate([m_new] * (kw // HEAD_DIM), axis=1)).astype(BF16))
            m_ref[r] = m_new
        pv = _dot(jnp.concatenate(ps, axis=0), vx)
        for r in range(n_rg):
            pr = pv[r * BLOCK:(r + 1) * BLOCK]
            acc_ref[r] = acc_ref[r] * alphas[r] + pr[:, :HEAD_DIM]
            l_ref[r] = l_ref[r] * alphas[r] + pr[:, HEAD_DIM:]

    def body(t, _):
        half(2 * t, 0, 1)
        half(2 * t + 1, 1, 0)
        return 0

    sc_ref[0] = scores(0)
    lax.fori_loop(0, (nc + 1) // 2, body, 0)
    for g in range(group):
        for t in range(qb):
            r = g * qb + t
            o_ref[t * BLOCK:(t + 1) * BLOCK, g * HEAD_DIM:(g + 1) * HEAD_DIM] = (
                acc_ref[r] / l_ref[r]).astype(o_ref.dtype)


DSA_Q_BLOCKS = 1


def _dsa(qkv, kr, bias, cos2, sin2, batch, seq, n_heads, n_kv, kw):
    nt = qkv.shape[0]
    qb = DSA_Q_BLOCKS if (seq // BLOCK) % DSA_Q_BLOCKS == 0 else 1
    tq = qb * BLOCK
    nq = seq // tq
    group = n_heads // n_kv
    n_rg = group * qb
    n_chunks = seq // kw
    assert n_chunks % 2 == 0
    vcol = n_heads + n_kv
    return pl.pallas_call(
        functools.partial(_dsa_kernel, group=group, qb=qb, kw=kw, scale=HEAD_DIM ** -0.5),
        grid=(batch, n_kv, nq),
        in_specs=[
            pl.BlockSpec((tq, group * HEAD_DIM), lambda b, h, i: (b * nq + i, h)),
            pl.BlockSpec((1, 1, n_chunks, HEAD_DIM, kw), lambda b, h, i: (b, h, 0, 0, 0)),
            pl.BlockSpec((seq, HEAD_DIM), lambda b, h, i: (b, vcol + h)),
            pl.BlockSpec((qb, n_chunks, BLOCK, kw), lambda b, h, i: (b * nq + i, 0, 0, 0)),
            pl.BlockSpec((tq, HEAD_DIM), lambda b, h, i: (b * nq + i, 0)),
            pl.BlockSpec((tq, HEAD_DIM), lambda b, h, i: (b * nq + i, 0)),
        ],
        out_specs=pl.BlockSpec((tq, group * HEAD_DIM), lambda b, h, i: (b * nq + i, h)),
        out_shape=jax.ShapeDtypeStruct((nt, n_heads * HEAD_DIM), BF16),
        scratch_shapes=[
            pltpu.VMEM((n_rg * BLOCK, HEAD_DIM), BF16),
            pltpu.VMEM((n_rg, BLOCK, HEAD_DIM), F32),
            pltpu.VMEM((n_rg, BLOCK, HEAD_DIM), F32),
            pltpu.VMEM((n_rg, BLOCK, HEAD_DIM), F32),
            pltpu.VMEM((2, n_rg * BLOCK, kw), F32),
        ],
        compiler_params=_params("arbitrary", "arbitrary", "arbitrary"),
        name="dsa_attn",
    )(qkv, kr, qkv, bias, cos2, sin2)


def _rope_tables(positions):
    inv_freq = ROPE_THETA ** (-jnp.arange(0, HEAD_DIM, 2, dtype=F32) / HEAD_DIM)
    ang = positions.astype(F32).reshape(-1)[:, None] * inv_freq
    cos, sin = jnp.cos(ang), jnp.sin(ang)
    return jnp.concatenate([cos, cos], axis=-1), jnp.concatenate([-sin, sin], axis=-1)


def _pick_tile(n, prefs):
    for t in prefs:
        if n % t == 0:
            return t
    return n


def kernel(x, c, positions, norm1_g, norm2_g, ada_w, ada_b, even_w_in, even_sinks, even_w_o,
           odd_w_in, odd_idx_k_g, odd_w_o, ffn_w_gate, ffn_w_up, ffn_conv_w, ffn_conv_b,
           ffn_w_down, final_g):
    batch, seq, d = x.shape
    depth = norm1_g.shape[0]
    nt = batch * seq
    a_heads = even_sinks.shape[1]
    a_kv = max(1, a_heads // 8)
    b_heads = even_w_o.shape[1] // HEAD_DIM - a_heads
    c_heads = odd_w_o.shape[1] // HEAD_DIM
    c_kv = c_heads // 4
    odd_in = odd_w_in.shape[2]
    n_ih = (odd_in - (c_heads + 2 * c_kv) * HEAD_DIM - HEAD_DIM) // (HEAD_DIM + 1)
    topk = min(TOPK_MAX, seq // 4)
    kw = min(512, seq)

    cos2, sin2 = _rope_tables(positions)
    c_pad = jnp.zeros((8, d), F32).at[:batch].set(c)
    mods = _ada(c_pad, ada_w, ada_b, tn=_pick_tile(6 * d, (512, 256, 128)))[:, :batch]
    mods = mods.reshape(depth, batch, 6, 1, d)

    tm = 1024
    w_down = ffn_w_down.astype(BF16)
    h = x.reshape(nt, d)
    for layer in range(depth):
        shift1, scale1, gate1, shift2, scale2, gate2 = [mods[layer, :, k] for k in range(6)]
        j = layer // 2
        hn = _norm_mod(h, norm1_g[layer], scale1, shift1, seq)
        if layer % 2 == 0:
            w_in = even_w_in
            n_in = w_in.shape[2]
            proj = _matmul([hn], w_in, n_in, 0, tm, _pick_tile(n_in, (512, 256, 128)), BF16, "in_proj_even",
                           layer=j)
            oa = _swa(proj, cos2, sin2, even_sinks[j], batch, seq, a_heads, a_kv)
            qcol = a_heads + 2 * a_kv
            kt = _heads_t(proj, qcol + b_heads, b_heads, cos2, sin2, batch, seq, min(SB_TILE, seq), False,
                          "sb_kt")
            ob = _sb(proj, kt, batch, seq, b_heads, qcol, qcol + 2 * b_heads)
            w_o = even_w_o
            if a_heads == b_heads:
                parts = [oa, ob]
            else:
                parts = [jnp.concatenate([oa, ob], axis=-1)]
            h = _matmul(parts, w_o, d, 0, tm, _pick_tile(d, (512, 256, 128)), F32, "out_proj_even",
                        res=h, gate=gate1, seq=seq, layer=j)
        else:
            w_in = odd_w_in.astype(BF16)
            n_qkv = (c_heads + 2 * c_kv) * HEAD_DIM
            n_qi = n_ih * HEAD_DIM
            qkv = _matmul([hn], w_in, n_qkv, 0, tm, _pick_tile(n_qkv, (512, 256, 128)), BF16, "in_proj_qkv",
                          layer=j)
            tn_qi = _pick_tile(n_qi, [t for t in (512, 256, 128) if n_qkv % t == 0])
            qi = _matmul([hn], w_in, n_qi, n_qkv, tm, tn_qi, BF16, "in_proj_qi", layer=j)
            w_tail = jnp.zeros((1, d, 2 * HEAD_DIM), BF16).at[0, :, :HEAD_DIM + n_ih].set(
                w_in[j, :, n_qkv + n_qi:].astype(BF16))
            tail = _matmul([hn], w_tail, 2 * HEAD_DIM, 0, tm, 2 * HEAD_DIM, F32, "in_proj_idx")
            kr = _heads_t(qkv, c_heads, c_kv, cos2, sin2, batch, seq, kw, True, "rope_k")
            kir = _idx_key(tail, odd_idx_k_g[j], cos2, sin2)
            bias = _indexer(qi, kir, tail, cos2, sin2, batch, seq, n_ih, topk, kw)
            o = _dsa(qkv, kr, bias, cos2, sin2, batch, seq, c_heads, c_kv, kw)
            w_o = odd_w_o
            h = _matmul([o], w_o, d, 0, tm, _pick_tile(d, (512, 256, 128)), F32, "out_proj_odd",
                        res=h, gate=gate1, seq=seq, layer=j)
        hn = _norm_mod(h, norm2_g[layer], scale2, shift2, seq)
        a = _ffn_up(hn, ffn_w_gate, ffn_w_up, layer, ffn_conv_w[layer], ffn_conv_b[layer], seq)
        h = _matmul([a], w_down, d, 0, 1024, 256, F32, "ffn_down", res=h, gate=gate2, seq=seq, layer=layer,
                    single_buffer_x=True)
    return _norm_plain(h, final_g).reshape(batch, seq, d)
```

```python
import functools
import math

import jax
import jax.numpy as jnp
from jax import lax
from jax.experimental import pallas as pl
from jax.experimental.pallas import tpu as pltpu

F32 = jnp.float32
BF16 = jnp.bfloat16

HEAD_DIM = 128
BLOCK = 128
ROPE_THETA = 10000.0
NORM_EPS = 1e-6
TOPK_MAX = 256
NEG = -1e30
INT_MIN = -(2 ** 31)
VMEM_LIMIT_BYTES = 56 * 1024 * 1024


def _params(*sem):
    return pltpu.CompilerParams(dimension_semantics=sem, vmem_limit_bytes=VMEM_LIMIT_BYTES)


def _dot(a, b):
    return jnp.dot(a, b, preferred_element_type=F32)


def _dot_nt(a, b):
    return lax.dot_general(a, b, (((1,), (1,)), ((), ())), preferred_element_type=F32)


def _rope(x, cos2, sin2):
    return x * cos2 + pltpu.roll(x, HEAD_DIM // 2, axis=1) * sin2


def _ada_kernel(c_ref, w_ref, b_ref, o_ref):
    c = c_ref[...]
    ca = (c * jax.nn.sigmoid(c)).astype(BF16)
    o_ref[0] = _dot(ca, w_ref[0].astype(BF16)) + b_ref[0]


def _ada(c_pad, ada_w, ada_b, tn=512):
    n_layers, d, n = ada_w.shape
    rows = c_pad.shape[0]
    return pl.pallas_call(
        _ada_kernel,
        grid=(n_layers, n // tn),
        in_specs=[
            pl.BlockSpec((rows, d), lambda l, j: (0, 0)),
            pl.BlockSpec((1, d, tn), lambda l, j: (l, 0, j)),
            pl.BlockSpec((1, 1, tn), lambda l, j: (l, 0, j)),
        ],
        out_specs=pl.BlockSpec((1, rows, tn), lambda l, j: (l, 0, j)),
        out_shape=jax.ShapeDtypeStruct((n_layers, rows, n), F32),
        compiler_params=_params("arbitrary", "arbitrary"),
        name="ada_mod",
    )(c_pad, ada_w, ada_b.reshape(n_layers, 1, n))


def _norm_mod_kernel(x_ref, g_ref, sc_ref, sh_ref, o_ref):
    x = x_ref[...]
    y = x * lax.rsqrt(jnp.mean(x * x, axis=-1, keepdims=True) + NORM_EPS) * g_ref[...]
    o_ref[...] = (y * (1.0 + sc_ref[0]) + sh_ref[0]).astype(o_ref.dtype)


def _norm_plain_kernel(x_ref, g_ref, o_ref):
    x = x_ref[...]
    y = x * lax.rsqrt(jnp.mean(x * x, axis=-1, keepdims=True) + NORM_EPS) * g_ref[...]
    o_ref[...] = y.astype(o_ref.dtype)


def _norm_mod(h, g, scale, shift, seq, tr=512):
    nt, d = h.shape
    tr = min(tr, seq)
    bps = seq // tr
    return pl.pallas_call(
        _norm_mod_kernel,
        grid=(nt // tr,),
        in_specs=[
            pl.BlockSpec((tr, d), lambda i: (i, 0)),
            pl.BlockSpec((1, d), lambda i: (0, 0)),
            pl.BlockSpec((1, 1, d), lambda i: (i // bps, 0, 0)),
            pl.BlockSpec((1, 1, d), lambda i: (i // bps, 0, 0)),
        ],
        out_specs=pl.BlockSpec((tr, d), lambda i: (i, 0)),
        out_shape=jax.ShapeDtypeStruct((nt, d), BF16),
        compiler_params=_params("arbitrary"),
        name="norm_mod",
    )(h, g.reshape(1, d), scale, shift)


def _norm_plain(h, g, tr=512):
    nt, d = h.shape
    tr = min(tr, nt)
    return pl.pallas_call(
        _norm_plain_kernel,
        grid=(nt // tr,),
        in_specs=[pl.BlockSpec((tr, d), lambda i: (i, 0)), pl.BlockSpec((1, d), lambda i: (0, 0))],
        out_specs=pl.BlockSpec((tr, d), lambda i: (i, 0)),
        out_shape=jax.ShapeDtypeStruct((nt, d), F32),
        compiler_params=_params("arbitrary"),
        name="norm_final",
    )(h, g.reshape(1, d))


def _mm_kernel(*refs, n_parts, has_res):
    acc = None
    for x_ref, w_ref in zip(refs[:n_parts], refs[n_parts:2 * n_parts]):
        d = _dot(x_ref[...], w_ref[...].astype(BF16))
        acc = d if acc is None else acc + d
    rest = refs[2 * n_parts:]
    if has_res:
        res_ref, gate_ref, o_ref = rest
        o_ref[...] = res_ref[...] + gate_ref[0] * acc
    else:
        (o_ref,) = rest
        o_ref[...] = acc.astype(o_ref.dtype)


def _matmul(xs, w, n_out, col_off, tm, tn, out_dtype, name, res=None, gate=None, seq=None, layer=0,
            single_buffer_x=False):
    m, kp = xs[0].shape
    tm = min(tm, m if seq is None else seq)
    tn = min(tn, n_out)
    assert m % tm == 0 and n_out % tn == 0 and col_off % tn == 0
    cb = col_off // tn
    n_parts = len(xs)
    x_mode = dict(pipeline_mode=pl.Buffered(1)) if single_buffer_x else {}
    in_specs = [pl.BlockSpec((tm, kp), lambda i, j: (i, 0), **x_mode) for _ in xs]
    in_specs += [pl.BlockSpec((None, kp, tn), lambda i, j, p=p: (layer, p, j + cb)) for p in range(n_parts)]
    args = list(xs) + [w] * n_parts
    if res is not None:
        bps = seq // tm
        in_specs += [pl.BlockSpec((tm, tn), lambda i, j: (i, j)),
                     pl.BlockSpec((1, 1, tn), lambda i, j: (i // bps, 0, j))]
        args += [res, gate]
    return pl.pallas_call(
        functools.partial(_mm_kernel, n_parts=n_parts, has_res=res is not None),
        grid=(m // tm, n_out // tn),
        in_specs=in_specs,
        out_specs=pl.BlockSpec((tm, tn), lambda i, j: (i, j)),
        out_shape=jax.ShapeDtypeStruct((m, n_out), out_dtype),
        compiler_params=_params("arbitrary", "arbitrary"),
        name=name,
    )(*args)


HALO = 16


def _ffn_up_kernel(x_ref, xh_ref, wg_ref, wu_ref, cw_ref, cb_ref, o_ref, gs_ref, *, tm, bps):
    i = pl.program_id(0)
    x = x_ref[...]
    wg = wg_ref[...].astype(BF16)
    g = _dot(x, wg)
    u = _dot(x, wu_ref[...].astype(BF16))
    gs_ref[0:HALO, :] = jnp.where(i % bps == 0, 0.0, _dot(xh_ref[...], wg))
    gs_ref[HALO:, :] = g
    g1 = gs_ref[HALO - 1:HALO - 1 + tm, :]
    g2 = gs_ref[HALO - 2:HALO - 2 + tm, :]
    cw = cw_ref[...]
    gc = cw[0:1] * g2 + cw[1:2] * g1 + cw[2:3] * g + cb_ref[...]
    o_ref[...] = (gc * jax.nn.sigmoid(gc) * u).astype(o_ref.dtype)


def _ffn_up(hn, wg, wu, layer, conv_w, conv_b, seq, tm=1024, tn=256):
    m, d = hn.shape
    dff = wg.shape[2]
    tm = min(tm, seq)
    assert dff % tn == 0 and tm % HALO == 0
    bps = seq // tm
    hpb = tm // HALO
    return pl.pallas_call(
        functools.partial(_ffn_up_kernel, tm=tm, bps=bps),
        grid=(m // tm, dff // tn),
        in_specs=[
            pl.BlockSpec((tm, d), lambda i, j: (i, 0)),
            pl.BlockSpec((HALO, d), lambda i, j: (jnp.maximum(i * hpb - 1, 0), 0)),
            pl.BlockSpec((None, d, tn), lambda i, j: (layer, 0, j)),
            pl.BlockSpec((None, d, tn), lambda i, j: (layer, 0, j)),
            pl.BlockSpec((3, tn), lambda i, j: (0, j)),
            pl.BlockSpec((1, tn), lambda i, j: (0, j)),
        ],
        out_specs=pl.BlockSpec((tm, tn), lambda i, j: (i, j)),
        out_shape=jax.ShapeDtypeStruct((m, dff), BF16),
        scratch_shapes=[pltpu.VMEM((tm + HALO, tn), F32)],
        compiler_params=_params("arbitrary", "arbitrary"),
        name="ffn_up",
    )(hn, hn, wg, wu, conv_w, conv_b.reshape(1, dff))


def _swa_kernel(sink_ref, q_ref, kc_ref, kp_ref, vc_ref, vp_ref, cq_ref, sq_ref, cp_ref, sp_ref,
                o_ref, *, n_heads, n_kv, scale):
    i = pl.program_id(1)
    cq, sq, cp, sp = cq_ref[...], sq_ref[...], cp_ref[...], sp_ref[...]
    group = n_heads // n_kv
    qi = lax.broadcasted_iota(jnp.int32, (BLOCK, BLOCK), 0)
    kj = lax.broadcasted_iota(jnp.int32, (BLOCK, BLOCK), 1)
    mask = jnp.concatenate([(kj > qi) & (i > 0), kj <= qi], axis=1)
    for hk in range(n_kv):
        sl = slice(hk * HEAD_DIM, (hk + 1) * HEAD_DIM)
        k2 = jnp.concatenate([_rope(kp_ref[:, sl].astype(F32), cp, sp),
                              _rope(kc_ref[:, sl].astype(F32), cq, sq)], axis=0)
        k2t = k2.T.astype(BF16)
        v2 = jnp.concatenate([vp_ref[:, sl], vc_ref[:, sl]], axis=0)
        heads = [hk * group + g for g in range(group)]
        qs = jnp.concatenate(
            [_rope(q_ref[:, h * HEAD_DIM:(h + 1) * HEAD_DIM].astype(F32), cq, sq).astype(BF16)
             for h in heads], axis=0)
        s_all = _dot(qs, k2t) * scale
        ps, dens = [], []
        for g, h in enumerate(heads):
            s = jnp.where(mask, s_all[g * BLOCK:(g + 1) * BLOCK], NEG)
            sink = sink_ref[h]
            m = jnp.maximum(jnp.max(s, axis=-1, keepdims=True), sink)
            p = jnp.exp(s - m)
            dens.append(jnp.sum(p, axis=-1, keepdims=True) + jnp.exp(sink - m))
            ps.append(p.astype(BF16))
        o_all = _dot(jnp.concatenate(ps, axis=0), v2)
        for g, h in enumerate(heads):
            o_ref[:, h * HEAD_DIM:(h + 1) * HEAD_DIM] = (
                o_all[g * BLOCK:(g + 1) * BLOCK] / dens[g]).astype(o_ref.dtype)


def _swa(proj, cos2, sin2, sinks, batch, seq, n_heads, n_kv):
    nt = proj.shape[0]
    nq = seq // BLOCK
    qw, kw = n_heads * HEAD_DIM, n_kv * HEAD_DIM
    kcol = n_heads // n_kv
    vcol = kcol + 1
    cur = lambda b, i: b * nq + i
    prev = lambda b, i: b * nq + jnp.maximum(i - 1, 0)
    return pl.pallas_call(
        functools.partial(_swa_kernel, n_heads=n_heads, n_kv=n_kv, scale=HEAD_DIM ** -0.5),
        grid=(batch, nq),
        in_specs=[
            pl.BlockSpec(memory_space=pltpu.SMEM),
            pl.BlockSpec((BLOCK, qw), lambda b, i: (cur(b, i), 0)),
            pl.BlockSpec((BLOCK, kw), lambda b, i: (cur(b, i), kcol)),
            pl.BlockSpec((BLOCK, kw), lambda b, i: (prev(b, i), kcol)),
            pl.BlockSpec((BLOCK, kw), lambda b, i: (cur(b, i), vcol)),
            pl.BlockSpec((BLOCK, kw), lambda b, i: (prev(b, i), vcol)),
            pl.BlockSpec((BLOCK, HEAD_DIM), lambda b, i: (cur(b, i), 0)),
            pl.BlockSpec((BLOCK, HEAD_DIM), lambda b, i: (cur(b, i), 0)),
            pl.BlockSpec((BLOCK, HEAD_DIM), lambda b, i: (prev(b, i), 0)),
            pl.BlockSpec((BLOCK, HEAD_DIM), lambda b, i: (prev(b, i), 0)),
        ],
        out_specs=pl.BlockSpec((BLOCK, qw), lambda b, i: (cur(b, i), 0)),
        out_shape=jax.ShapeDtypeStruct((nt, qw), BF16),
        compiler_params=_params("arbitrary", "arbitrary"),
        name="swa_attn",
    )(sinks, proj, proj, proj, proj, proj, cos2, sin2, cos2, sin2)


SB_TILE = 256
SB_HEADS_PER_STEP = 4
SB_DROP_CAP = 104.0


def _sb_kernel(q_ref, kt_ref, v_ref, o_ref, acc_ref, run_ref, *, scale, tile, hp):
    i = pl.program_id(2)
    row = lax.broadcasted_iota(jnp.int32, (tile, tile), 0)
    col = lax.broadcasted_iota(jnp.int32, (tile, tile), 1)
    strict = col < row
    rr = lax.broadcasted_iota(jnp.int32, (tile, tile + HEAD_DIM), 0)
    cc = lax.broadcasted_iota(jnp.int32, (tile, tile + HEAD_DIM), 1)
    cum = ((rr > cc) | (cc >= tile)).astype(BF16)

    def head_step(h, j, diag):
        hs = slice(h * HEAD_DIM, (h + 1) * HEAD_DIM)
        off = pl.multiple_of(j * tile, tile)
        v = v_ref[pl.ds(off, tile), hs]
        z = _dot(q_ref[:, hs], kt_ref[0, h, j]) * scale
        soft = jnp.log(1.0 + jnp.exp(-jnp.abs(z)))
        log_beta = jnp.minimum(z, 0.0) - soft
        drop = jnp.maximum(z, 0.0) + soft
        if diag:
            drop = jnp.where(strict, drop, 0.0)
        hi = drop.astype(BF16)
        lo = (drop - hi.astype(F32)).astype(BF16)
        sums = _dot(hi, cum) + _dot(lo, cum)
        after = sums[:, :tile]
        if not diag:
            run = run_ref[h]
            after = after + jnp.concatenate([run] * (tile // HEAD_DIM), axis=1)
        w = jnp.exp(log_beta - after)
        if diag:
            w = jnp.where(strict, w, 0.0)
        pv = _dot(w.astype(BF16), v)
        if diag:
            acc_ref[h] = pv
            run = sums[:, tile:]
        else:
            acc_ref[h] += pv
            run = run + sums[:, tile:]
        run_ref[h] = run
        return jnp.min(run)

    def step(j, diag):
        least = head_step(0, j, diag)
        for h in range(1, hp):
            least = jnp.minimum(least, head_step(h, j, diag))
        return least

    def cond(st):
        return (st[0] >= 0) & (st[1] < SB_DROP_CAP)

    def body(st):
        return st[0] - 1, step(st[0], False)

    lax.while_loop(cond, body, (i - 1, step(i, True)))
    for h in range(hp):
        o_ref[:, h * HEAD_DIM:(h + 1) * HEAD_DIM] = acc_ref[h].astype(o_ref.dtype)


def _sb(proj, kt, batch, seq, n_heads, qcol, vcol):
    nt = proj.shape[0]
    tile = min(SB_TILE, seq)
    nq = seq // tile
    hp = math.gcd(math.gcd(qcol, vcol), math.gcd(n_heads, SB_HEADS_PER_STEP))
    w = hp * HEAD_DIM
    return pl.pallas_call(
        functools.partial(_sb_kernel, scale=HEAD_DIM ** -0.5, tile=tile, hp=hp),
        grid=(batch, n_heads // hp, nq),
        in_specs=[
            pl.BlockSpec((tile, w), lambda b, h, i: (b * nq + i, qcol // hp + h)),
            pl.BlockSpec((1, hp, nq, HEAD_DIM, tile), lambda b, h, i: (b, h, 0, 0, 0)),
            pl.BlockSpec((seq, w), lambda b, h, i: (b, vcol // hp + h)),
        ],
        out_specs=pl.BlockSpec((tile, w), lambda b, h, i: (b * nq + i, h)),
        out_shape=jax.ShapeDtypeStruct((nt, n_heads * HEAD_DIM), BF16),
        scratch_shapes=[pltpu.VMEM((hp, tile, HEAD_DIM), F32), pltpu.VMEM((hp, tile, HEAD_DIM), F32)],
        compiler_params=_params("arbitrary", "arbitrary", "arbitrary"),
        name="sb_attn",
    )(proj, kt, proj)


def _rope_heads_t_kernel(x_ref, c_ref, s_ref, o_ref, *, n_heads, rotate, kw, cps):
    for j in range(cps):
        rows = slice(j * kw, (j + 1) * kw)
        c, s = c_ref[rows, :], s_ref[rows, :]
        for h in range(n_heads):
            x = x_ref[rows, h * HEAD_DIM:(h + 1) * HEAD_DIM].astype(F32)
            if rotate:
                x = _rope(x, c, s)
            o_ref[0, h, j] = x.T.astype(o_ref.dtype)


def _heads_t(x, first_head, n_heads, cos2, sin2, batch, seq, kw, rotate, name):
    hg = math.gcd(first_head, n_heads)
    n_chunks = seq // kw
    cps = math.gcd(n_chunks, max(1, 1024 // kw))
    steps = n_chunks // cps
    return pl.pallas_call(
        functools.partial(_rope_heads_t_kernel, n_heads=hg, rotate=rotate, kw=kw, cps=cps),
        grid=(batch, steps, n_heads // hg),
        in_specs=[
            pl.BlockSpec((cps * kw, hg * HEAD_DIM), lambda b, c, g: (b * steps + c, first_head // hg + g)),
            pl.BlockSpec((cps * kw, HEAD_DIM), lambda b, c, g: (b * steps + c, 0)),
            pl.BlockSpec((cps * kw, HEAD_DIM), lambda b, c, g: (b * steps + c, 0)),
        ],
        out_specs=pl.BlockSpec((1, hg, cps, HEAD_DIM, kw), lambda b, c, g: (b, g, c, 0, 0)),
        out_shape=jax.ShapeDtypeStruct((batch, n_heads, n_chunks, HEAD_DIM, kw), BF16),
        compiler_params=_params("arbitrary", "arbitrary", "arbitrary"),
        name=name,
    )(x, cos2, sin2)


def _idx_key_kernel(t_ref, g_ref, c_ref, s_ref, o_ref):
    x = t_ref[:, 0:HEAD_DIM]
    y = x * lax.rsqrt(jnp.mean(x * x, axis=-1, keepdims=True) + NORM_EPS) * g_ref[...]
    o_ref[...] = _rope(y, c_ref[...], s_ref[...]).astype(o_ref.dtype)


def _idx_key(tail, g, cos2, sin2, tr=512):
    nt, tw = tail.shape
    tr = min(tr, nt)
    return pl.pallas_call(
        _idx_key_kernel,
        grid=(nt // tr,),
        in_specs=[
            pl.BlockSpec((tr, tw), lambda i: (i, 0)),
            pl.BlockSpec((1, HEAD_DIM), lambda i: (0, 0)),
            pl.BlockSpec((tr, HEAD_DIM), lambda i: (i, 0)),
            pl.BlockSpec((tr, HEAD_DIM), lambda i: (i, 0)),
        ],
        out_specs=pl.BlockSpec((tr, HEAD_DIM), lambda i: (i, 0)),
        out_shape=jax.ShapeDtypeStruct((nt, HEAD_DIM), BF16),
        compiler_params=_params("arbitrary"),
        name="idx_key",
    )(tail, g.reshape(1, HEAD_DIM), cos2, sin2)


def _idx_kernel(qi_ref, kir_ref, tail_ref, c_ref, s_ref, o_ref, qr_ref, wb_ref, sk_ref, run_ref,
                *, n_ih, kw, topk, idx_scale, n_chunks):
    i = pl.program_id(1)
    cq, sq = c_ref[...], s_ref[...]
    wi = tail_ref[:, HEAD_DIM:2 * HEAD_DIM]
    for h in range(n_ih):
        hs = slice(h * BLOCK, (h + 1) * BLOCK)
        qr_ref[hs, :] = _rope(qi_ref[:, hs].astype(F32), cq, sq).astype(BF16)
        wb_ref[h] = jnp.broadcast_to(wi[:, h:h + 1], (BLOCK, HEAD_DIM))
    nc = ((i + 1) * BLOCK + kw - 1) // kw
    n_lt = kw // HEAD_DIM
    qidx = i * BLOCK + lax.broadcasted_iota(jnp.int32, (BLOCK, kw), 0)
    kloc = lax.broadcasted_iota(jnp.int32, (BLOCK, kw), 1)

    def score_chunk(c, _):
        off = pl.multiple_of(c * kw, kw)
        kc = kir_ref[pl.ds(off, kw), :]
        lg = _dot_nt(qr_ref[...], kc)
        accs = [jnp.zeros((BLOCK, HEAD_DIM), F32) for _ in range(n_lt)]
        for h in range(n_ih):
            w = wb_ref[h]
            for t in range(n_lt):
                tile = lg[h * BLOCK:(h + 1) * BLOCK, t * HEAD_DIM:(t + 1) * HEAD_DIM]
                accs[t] = accs[t] + jnp.maximum(tile, 0.0) * w
        acc = jnp.concatenate(accs, axis=1)
        score = jnp.where(c * kw + kloc <= qidx, acc * idx_scale, -jnp.inf)
        bits = lax.bitcast_convert_type(score, jnp.int32)
        bits = jnp.where(bits == INT_MIN, 0, bits)
        sk_ref[c] = jnp.where(bits < 0, bits ^ 0x7FFFFFFF, bits)
        return 0

    lax.fori_loop(0, nc, score_chunk, 0)

    def count_ge(cand):
        cb = jnp.broadcast_to(cand, (BLOCK, HEAD_DIM))

        def body(c, acc):
            x = sk_ref[c]
            for t in range(n_lt):
                acc = acc + jnp.where(x[:, t * HEAD_DIM:(t + 1) * HEAD_DIM] >= cb, 1.0, 0.0)
            return acc

        acc = lax.fori_loop(0, nc, body, jnp.zeros((BLOCK, HEAD_DIM), F32))
        return jnp.sum(acc, axis=-1, keepdims=True)

    zero = jnp.zeros((BLOCK, 1), jnp.int32)
    thr = jnp.where(count_ge(zero) >= topk, zero, INT_MIN)

    def bit_step(t, thr):
        cand = thr | (jnp.int32(1) << (30 - t))
        return jnp.where(count_ge(cand) >= topk, cand, thr)

    thr = lax.fori_loop(0, 31, bit_step, thr)
    tb = jnp.broadcast_to(thr, (BLOCK, kw))
    has_ties = jnp.max(count_ge(thr)) > topk

    @pl.when(jnp.logical_not(has_ties))
    def _():
        for c in range(n_chunks):
            @pl.when(c < nc)
            def _():
                sel = (sk_ref[c] >= tb) & (c * kw + kloc <= qidx)
                o_ref[0, c] = jnp.where(sel, 0.0, NEG).astype(o_ref.dtype)

    @pl.when(has_ties)
    def _():
        need = jnp.broadcast_to(topk - count_ge(thr + 1), (BLOCK, kw)).astype(F32)
        rr = lax.broadcasted_iota(jnp.int32, (kw, kw + HEAD_DIM), 0)
        cc = lax.broadcasted_iota(jnp.int32, (kw, kw + HEAD_DIM), 1)
        before = ((rr < cc) | (cc >= kw)).astype(BF16)
        run_ref[...] = jnp.zeros((BLOCK, HEAD_DIM), F32)
        for c in range(n_chunks):
            @pl.when(c < nc)
            def _():
                x = sk_ref[c]
                causal = c * kw + kloc <= qidx
                tied = (x == tb) & causal
                sums = _dot(jnp.where(tied, 1.0, 0.0).astype(BF16), before)
                run = run_ref[...]
                rank = sums[:, :kw] + jnp.concatenate([run] * n_lt, axis=1)
                sel = ((x > tb) & causal) | (tied & (rank < need))
                o_ref[0, c] = jnp.where(sel, 0.0, NEG).astype(o_ref.dtype)
                run_ref[...] = run + sums[:, kw:]

    for c in range(n_chunks):
        @pl.when(c >= nc)
        def _():
            o_ref[0, c] = jnp.full((BLOCK, kw), NEG, o_ref.dtype)


def _indexer(qi, kir, tail, cos2, sin2, batch, seq, n_ih, topk, kw):
    nt = qi.shape[0]
    nq = seq // BLOCK
    n_chunks = seq // kw
    idx_scale = (n_ih ** -0.5) * (HEAD_DIM ** -0.5)
    return pl.pallas_call(
        functools.partial(_idx_kernel, n_ih=n_ih, kw=kw, topk=topk, idx_scale=idx_scale,
                          n_chunks=n_chunks),
        grid=(batch, nq),
        in_specs=[
            pl.BlockSpec((BLOCK, n_ih * HEAD_DIM), lambda b, i: (b * nq + i, 0)),
            pl.BlockSpec((seq, HEAD_DIM), lambda b, i: (b, 0)),
            pl.BlockSpec((BLOCK, 2 * HEAD_DIM), lambda b, i: (b * nq + i, 0)),
            pl.BlockSpec((BLOCK, HEAD_DIM), lambda b, i: (b * nq + i, 0)),
            pl.BlockSpec((BLOCK, HEAD_DIM), lambda b, i: (b * nq + i, 0)),
        ],
        out_specs=pl.BlockSpec((1, n_chunks, BLOCK, kw), lambda b, i: (b * nq + i, 0, 0, 0)),
        out_shape=jax.ShapeDtypeStruct((batch * nq, n_chunks, BLOCK, kw), BF16),
        scratch_shapes=[
            pltpu.VMEM((n_ih * BLOCK, HEAD_DIM), BF16),
            pltpu.VMEM((n_ih, BLOCK, HEAD_DIM), F32),
            pltpu.VMEM((n_chunks, BLOCK, kw), jnp.int32),
            pltpu.VMEM((BLOCK, HEAD_DIM), F32),
        ],
        compiler_params=_params("arbitrary", "arbitrary"),
        name="dsa_indexer",
    )(qi, kir, tail, cos2, sin2)


LOG2E = 1.4426950408889634


def _dsa_kernel(q_ref, kt_ref, v_ref, bias_ref, c_ref, s_ref, o_ref, qs_ref, m_ref, l_ref, acc_ref, sc_ref,
                *, group, qb, kw, scale):
    i = pl.program_id(2)
    n_rg = group * qb
    for g in range(group):
        for t in range(qb):
            r = g * qb + t
            rows = slice(t * BLOCK, (t + 1) * BLOCK)
            qg = _rope(q_ref[rows, g * HEAD_DIM:(g + 1) * HEAD_DIM].astype(F32), c_ref[rows, :], s_ref[rows, :])
            qs_ref[r * BLOCK:(r + 1) * BLOCK, :] = (qg * (scale * LOG2E)).astype(BF16)
            m_ref[r] = jnp.full((BLOCK, HEAD_DIM), NEG, F32)
            l_ref[r] = jnp.zeros((BLOCK, HEAD_DIM), F32)
            acc_ref[r] = jnp.zeros((BLOCK, HEAD_DIM), F32)
    nc = ((i + 1) * qb * BLOCK + kw - 1) // kw
    ones = jnp.ones((kw, HEAD_DIM), BF16)

    def scores(c):
        return _dot(qs_ref[...], kt_ref[0, 0, c])

    def half(c, src, dst):
        s_all = sc_ref[src]
        sc_ref[dst] = scores(jnp.minimum(c + 1, kt_ref.shape[2] - 1))
        off = pl.multiple_of(c * kw, kw)
        vx = jnp.concatenate([v_ref[pl.ds(off, kw), :], ones], axis=1)
        bs = [bias_ref[t, c].astype(F32) for t in range(qb)]
        ps, alphas = [], []
        for r in range(n_rg):
            s = s_all[r * BLOCK:(r + 1) * BLOCK] + bs[r % qb]
            m_old = m_ref[r]
            m_new = jnp.maximum(m_old, jnp.max(s, axis=-1, keepdims=True))
            alphas.append(jnp.exp2(m_old - m_new))
            ps.append(jnp.exp2(s - jnp.concatenate([m_new] * (kw // HEAD_DIM), axis=1)).astype(BF16))
            m_ref[r] = m_new
        pv = _dot(jnp.concatenate(ps, axis=0), vx)
        for r in range(n_rg):
            pr = pv[r * BLOCK:(r + 1) * BLOCK]
            acc_ref[r] = acc_ref[r] * alphas[r] + pr[:, :HEAD_DIM]
            l_ref[r] = l_ref[r] * alphas[r] + pr[:, HEAD_DIM:]

    n_pairs = (nc + 1) // 2

    def two_pairs(t, _):
        for u in range(4):
            half(4 * t + u, u % 2, 1 - u % 2)
        return 0

    sc_ref[0] = scores(0)
    lax.fori_loop(0, n_pairs // 2, two_pairs, 0)

    @pl.when(n_pairs % 2 == 1)
    def _():
        half(2 * n_pairs - 2, 0, 1)
        half(2 * n_pairs - 1, 1, 0)
    for g in range(group):
        for t in range(qb):
            r = g * qb + t
            o_ref[t * BLOCK:(t + 1) * BLOCK, g * HEAD_DIM:(g + 1) * HEAD_DIM] = (
                acc_ref[r] / l_ref[r]).astype(o_ref.dtype)


DSA_Q_BLOCKS = 1


def _dsa(qkv, kr, bias, cos2, sin2, batch, seq, n_heads, n_kv, kw):
    nt = qkv.shape[0]
    qb = DSA_Q_BLOCKS if (seq // BLOCK) % DSA_Q_BLOCKS == 0 else 1
    tq = qb * BLOCK
    nq = seq // tq
    group = n_heads // n_kv
    n_rg = group * qb
    n_chunks = seq // kw
    assert n_chunks % 2 == 0
    vcol = n_heads + n_kv
    return pl.pallas_call(
        functools.partial(_dsa_kernel, group=group, qb=qb, kw=kw, scale=HEAD_DIM ** -0.5),
        grid=(batch, n_kv, nq),
        in_specs=[
            pl.BlockSpec((tq, group * HEAD_DIM), lambda b, h, i: (b * nq + i, h)),
            pl.BlockSpec((1, 1, n_chunks, HEAD_DIM, kw), lambda b, h, i: (b, h, 0, 0, 0)),
            pl.BlockSpec((seq, HEAD_DIM), lambda b, h, i: (b, vcol + h)),
            pl.BlockSpec((qb, n_chunks, BLOCK, kw), lambda b, h, i: (b * nq + i, 0, 0, 0)),
            pl.BlockSpec((tq, HEAD_DIM), lambda b, h, i: (b * nq + i, 0)),
            pl.BlockSpec((tq, HEAD_DIM), lambda b, h, i: (b * nq + i, 0)),
        ],
        out_specs=pl.BlockSpec((tq, group * HEAD_DIM), lambda b, h, i: (b * nq + i, h)),
        out_shape=jax.ShapeDtypeStruct((nt, n_heads * HEAD_DIM), BF16),
        scratch_shapes=[
            pltpu.VMEM((n_rg * BLOCK, HEAD_DIM), BF16),
            pltpu.VMEM((n_rg, BLOCK, HEAD_DIM), F32),
            pltpu.VMEM((n_rg, BLOCK, HEAD_DIM), F32),
            pltpu.VMEM((n_rg, BLOCK, HEAD_DIM), F32),
            pltpu.VMEM((2, n_rg * BLOCK, kw), F32),
        ],
        compiler_params=_params("arbitrary", "arbitrary", "arbitrary"),
        name="dsa_attn",
    )(qkv, kr, qkv, bias, cos2, sin2)


def _rope_tables(positions):
    inv_freq = ROPE_THETA ** (-jnp.arange(0, HEAD_DIM, 2, dtype=F32) / HEAD_DIM)
    ang = positions.astype(F32).reshape(-1)[:, None] * inv_freq
    cos, sin = jnp.cos(ang), jnp.sin(ang)
    return jnp.concatenate([cos, cos], axis=-1), jnp.concatenate([-sin, sin], axis=-1)


def _pick_tile(n, prefs):
    for t in prefs:
        if n % t == 0:
            return t
    return n


def kernel(x, c, positions, norm1_g, norm2_g, ada_w, ada_b, even_w_in, even_sinks, even_w_o,
           odd_w_in, odd_idx_k_g, odd_w_o, ffn_w_gate, ffn_w_up, ffn_conv_w, ffn_conv_b,
           ffn_w_down, final_g):
    batch, seq, d = x.shape
    depth = norm1_g.shape[0]
    nt = batch * seq
    a_heads = even_sinks.shape[1]
    a_kv = max(1, a_heads // 8)
    b_heads = even_w_o.shape[1] // HEAD_DIM - a_heads
    c_heads = odd_w_o.shape[1] // HEAD_DIM
    c_kv = c_heads // 4
    odd_in = odd_w_in.shape[2]
    n_ih = (odd_in - (c_heads + 2 * c_kv) * HEAD_DIM - HEAD_DIM) // (HEAD_DIM + 1)
    topk = min(TOPK_MAX, seq // 4)
    kw = min(512, seq)

    cos2, sin2 = _rope_tables(positions)
    c_pad = jnp.zeros((8, d), F32).at[:batch].set(c)
    mods = _ada(c_pad, ada_w, ada_b, tn=_pick_tile(6 * d, (512, 256, 128)))[:, :batch]
    mods = mods.reshape(depth, batch, 6, 1, d)

    tm = 1024
    w_down = ffn_w_down.astype(BF16)
    h = x.reshape(nt, d)
    for layer in range(depth):
        shift1, scale1, gate1, shift2, scale2, gate2 = [mods[layer, :, k] for k in range(6)]
        j = layer // 2
        hn = _norm_mod(h, norm1_g[layer], scale1, shift1, seq)
        if layer % 2 == 0:
            w_in = even_w_in
            n_in = w_in.shape[2]
            proj = _matmul([hn], w_in, n_in, 0, tm, _pick_tile(n_in, (512, 256, 128)), BF16, "in_proj_even",
                           layer=j)
            oa = _swa(proj, cos2, sin2, even_sinks[j], batch, seq, a_heads, a_kv)
            qcol = a_heads + 2 * a_kv
            kt = _heads_t(proj, qcol + b_heads, b_heads, cos2, sin2, batch, seq, min(SB_TILE, seq), False,
                          "sb_kt")
            ob = _sb(proj, kt, batch, seq, b_heads, qcol, qcol + 2 * b_heads)
            w_o = even_w_o
            if a_heads == b_heads:
                parts = [oa, ob]
            else:
                parts = [jnp.concatenate([oa, ob], axis=-1)]
            h = _matmul(parts, w_o, d, 0, tm, _pick_tile(d, (512, 256, 128)), F32, "out_proj_even",
                        res=h, gate=gate1, seq=seq, layer=j)
        else:
            w_in = odd_w_in.astype(BF16)
            n_qkv = (c_heads + 2 * c_kv) * HEAD_DIM
            n_qi = n_ih * HEAD_DIM
            qkv = _matmul([hn], w_in, n_qkv, 0, tm, _pick_tile(n_qkv, (512, 256, 128)), BF16, "in_proj_qkv",
                          layer=j)
            tn_qi = _pick_tile(n_qi, [t for t in (512, 256, 128) if n_qkv % t == 0])
            qi = _matmul([hn], w_in, n_qi, n_qkv, tm, tn_qi, BF16, "in_proj_qi", layer=j)
            w_tail = jnp.zeros((1, d, 2 * HEAD_DIM), BF16).at[0, :, :HEAD_DIM + n_ih].set(
                w_in[j, :, n_qkv + n_qi:].astype(BF16))
            tail = _matmul([hn], w_tail, 2 * HEAD_DIM, 0, tm, 2 * HEAD_DIM, F32, "in_proj_idx")
            kr = _heads_t(qkv, c_heads, c_kv, cos2, sin2, batch, seq, kw, True, "rope_k")
            kir = _idx_key(tail, odd_idx_k_g[j], cos2, sin2)
            bias = _indexer(qi, kir, tail, cos2, sin2, batch, seq, n_ih, topk, kw)
            o = _dsa(qkv, kr, bias, cos2, sin2, batch, seq, c_heads, c_kv, kw)
            w_o = odd_w_o
            h = _matmul([o], w_o, d, 0, tm, _pick_tile(d, (512, 256, 128)), F32, "out_proj_odd",
                        res=h, gate=gate1, seq=seq, layer=j)
        hn = _norm_mod(h, norm2_g[layer], scale2, shift2, seq)
        a = _ffn_up(hn, ffn_w_gate, ffn_w_up, layer, ffn_conv_w[layer], ffn_conv_b[layer], seq)
        h = _matmul([a], w_down, d, 0, 1024, 256, F32, "ffn_down", res=h, gate=gate2, seq=seq, layer=layer,
                    single_buffer_x=True)
    return _norm_plain(h, final_g).reshape(batch, seq, d)
```

```python
import functools
import math

import jax
import jax.numpy as jnp
from jax import lax
from jax.experimental import pallas as pl
from jax.experimental.pallas import tpu as pltpu

F32 = jnp.float32
BF16 = jnp.bfloat16

HEAD_DIM = 128
BLOCK = 128
ROPE_THETA = 10000.0
NORM_EPS = 1e-6
TOPK_MAX = 256
NEG = -1e30
INT_MIN = -(2 ** 31)
VMEM_LIMIT_BYTES = 56 * 1024 * 1024


def _params(*sem):
    return pltpu.CompilerParams(dimension_semantics=sem, vmem_limit_bytes=VMEM_LIMIT_BYTES)


def _dot(a, b):
    return jnp.dot(a, b, preferred_element_type=F32)


def _dot_nt(a, b):
    return lax.dot_general(a, b, (((1,), (1,)), ((), ())), preferred_element_type=F32)


def _rope(x, cos2, sin2):
    return x * cos2 + pltpu.roll(x, HEAD_DIM // 2, axis=1) * sin2


def _ada_kernel(c_ref, w_ref, b_ref, o_ref):
    c = c_ref[...]
    ca = (c * jax.nn.sigmoid(c)).astype(BF16)
    o_ref[0] = _dot(ca, w_ref[0].astype(BF16)) + b_ref[0]


def _ada(c_pad, ada_w, ada_b, tn=512):
    n_layers, d, n = ada_w.shape
    rows = c_pad.shape[0]
    return pl.pallas_call(
        _ada_kernel,
        grid=(n_layers, n // tn),
        in_specs=[
            pl.BlockSpec((rows, d), lambda l, j: (0, 0)),
            pl.BlockSpec((1, d, tn), lambda l, j: (l, 0, j)),
            pl.BlockSpec((1, 1, tn), lambda l, j: (l, 0, j)),
        ],
        out_specs=pl.BlockSpec((1, rows, tn), lambda l, j: (l, 0, j)),
        out_shape=jax.ShapeDtypeStruct((n_layers, rows, n), F32),
        compiler_params=_params("arbitrary", "arbitrary"),
        name="ada_mod",
    )(c_pad, ada_w, ada_b.reshape(n_layers, 1, n))


def _norm_mod_kernel(x_ref, g_ref, sc_ref, sh_ref, o_ref):
    x = x_ref[...]
    y = x * lax.rsqrt(jnp.mean(x * x, axis=-1, keepdims=True) + NORM_EPS) * g_ref[...]
    o_ref[...] = (y * (1.0 + sc_ref[0]) + sh_ref[0]).astype(o_ref.dtype)


def _norm_plain_kernel(x_ref, g_ref, o_ref):
    x = x_ref[...]
    y = x * lax.rsqrt(jnp.mean(x * x, axis=-1, keepdims=True) + NORM_EPS) * g_ref[...]
    o_ref[...] = y.astype(o_ref.dtype)


def _norm_mod(h, g, scale, shift, seq, tr=512):
    nt, d = h.shape
    tr = min(tr, seq)
    bps = seq // tr
    return pl.pallas_call(
        _norm_mod_kernel,
        grid=(nt // tr,),
        in_specs=[
            pl.BlockSpec((tr, d), lambda i: (i, 0)),
            pl.BlockSpec((1, d), lambda i: (0, 0)),
            pl.BlockSpec((1, 1, d), lambda i: (i // bps, 0, 0)),
            pl.BlockSpec((1, 1, d), lambda i: (i // bps, 0, 0)),
        ],
        out_specs=pl.BlockSpec((tr, d), lambda i: (i, 0)),
        out_shape=jax.ShapeDtypeStruct((nt, d), BF16),
        compiler_params=_params("arbitrary"),
        name="norm_mod",
    )(h, g.reshape(1, d), scale, shift)


def _norm_plain(h, g, tr=512):
    nt, d = h.shape
    tr = min(tr, nt)
    return pl.pallas_call(
        _norm_plain_kernel,
        grid=(nt // tr,),
        in_specs=[pl.BlockSpec((tr, d), lambda i: (i, 0)), pl.BlockSpec((1, d), lambda i: (0, 0))],
        out_specs=pl.BlockSpec((tr, d), lambda i: (i, 0)),
        out_shape=jax.ShapeDtypeStruct((nt, d), F32),
        compiler_params=_params("arbitrary"),
        name="norm_final",
    )(h, g.reshape(1, d))


def _mm_kernel(*refs, n_parts, has_res):
    acc = None
    for x_ref, w_ref in zip(refs[:n_parts], refs[n_parts:2 * n_parts]):
        d = _dot(x_ref[...], w_ref[...].astype(BF16))
        acc = d if acc is None else acc + d
    rest = refs[2 * n_parts:]
    if has_res:
        res_ref, gate_ref, o_ref = rest
        o_ref[...] = res_ref[...] + gate_ref[0] * acc
    else:
        (o_ref,) = rest
        o_ref[...] = acc.astype(o_ref.dtype)


def _matmul(xs, w, n_out, col_off, tm, tn, out_dtype, name, res=None, gate=None, seq=None, layer=0,
            single_buffer_x=False):
    m, kp = xs[0].shape
    tm = min(tm, m if seq is None else seq)
    tn = min(tn, n_out)
    assert m % tm == 0 and n_out % tn == 0 and col_off % tn == 0
    cb = col_off // tn
    n_parts = len(xs)
    x_mode = dict(pipeline_mode=pl.Buffered(1)) if single_buffer_x else {}
    in_specs = [pl.BlockSpec((tm, kp), lambda i, j: (i, 0), **x_mode) for _ in xs]
    in_specs += [pl.BlockSpec((None, kp, tn), lambda i, j, p=p: (layer, p, j + cb)) for p in range(n_parts)]
    args = list(xs) + [w] * n_parts
    if res is not None:
        bps = seq // tm
        in_specs += [pl.BlockSpec((tm, tn), lambda i, j: (i, j)),
                     pl.BlockSpec((1, 1, tn), lambda i, j: (i // bps, 0, j))]
        args += [res, gate]
    return pl.pallas_call(
        functools.partial(_mm_kernel, n_parts=n_parts, has_res=res is not None),
        grid=(m // tm, n_out // tn),
        in_specs=in_specs,
        out_specs=pl.BlockSpec((tm, tn), lambda i, j: (i, j)),
        out_shape=jax.ShapeDtypeStruct((m, n_out), out_dtype),
        compiler_params=_params("arbitrary", "arbitrary"),
        name=name,
    )(*args)


HALO = 16


def _ffn_up_kernel(x_ref, xh_ref, wg_ref, wu_ref, cw_ref, cb_ref, o_ref, gs_ref, *, tm, bps):
    i = pl.program_id(0)
    x = x_ref[...]
    wg = wg_ref[...].astype(BF16)
    g = _dot(x, wg)
    u = _dot(x, wu_ref[...].astype(BF16))
    gs_ref[0:HALO, :] = jnp.where(i % bps == 0, 0.0, _dot(xh_ref[...], wg))
    gs_ref[HALO:, :] = g
    g1 = gs_ref[HALO - 1:HALO - 1 + tm, :]
    g2 = gs_ref[HALO - 2:HALO - 2 + tm, :]
    cw = cw_ref[...]
    gc = cw[0:1] * g2 + cw[1:2] * g1 + cw[2:3] * g + cb_ref[...]
    o_ref[...] = (gc * jax.nn.sigmoid(gc) * u).astype(o_ref.dtype)


def _ffn_up(hn, wg, wu, layer, conv_w, conv_b, seq, tm=1024, tn=256):
    m, d = hn.shape
    dff = wg.shape[2]
    tm = min(tm, seq)
    assert dff % tn == 0 and tm % HALO == 0
    bps = seq // tm
    hpb = tm // HALO
    return pl.pallas_call(
        functools.partial(_ffn_up_kernel, tm=tm, bps=bps),
        grid=(m // tm, dff // tn),
        in_specs=[
            pl.BlockSpec((tm, d), lambda i, j: (i, 0)),
            pl.BlockSpec((HALO, d), lambda i, j: (jnp.maximum(i * hpb - 1, 0), 0)),
            pl.BlockSpec((None, d, tn), lambda i, j: (layer, 0, j)),
            pl.BlockSpec((None, d, tn), lambda i, j: (layer, 0, j)),
            pl.BlockSpec((3, tn), lambda i, j: (0, j)),
            pl.BlockSpec((1, tn), lambda i, j: (0, j)),
        ],
        out_specs=pl.BlockSpec((tm, tn), lambda i, j: (i, j)),
        out_shape=jax.ShapeDtypeStruct((m, dff), BF16),
        scratch_shapes=[pltpu.VMEM((tm + HALO, tn), F32)],
        compiler_params=_params("arbitrary", "arbitrary"),
        name="ffn_up",
    )(hn, hn, wg, wu, conv_w, conv_b.reshape(1, dff))


def _swa_kernel(sink_ref, q_ref, kc_ref, kp_ref, vc_ref, vp_ref, cq_ref, sq_ref, cp_ref, sp_ref,
                o_ref, *, n_heads, n_kv, scale):
    i = pl.program_id(1)
    cq, sq, cp, sp = cq_ref[...], sq_ref[...], cp_ref[...], sp_ref[...]
    group = n_heads // n_kv
    qi = lax.broadcasted_iota(jnp.int32, (BLOCK, BLOCK), 0)
    kj = lax.broadcasted_iota(jnp.int32, (BLOCK, BLOCK), 1)
    mask = jnp.concatenate([(kj > qi) & (i > 0), kj <= qi], axis=1)
    for hk in range(n_kv):
        sl = slice(hk * HEAD_DIM, (hk + 1) * HEAD_DIM)
        k2 = jnp.concatenate([_rope(kp_ref[:, sl].astype(F32), cp, sp),
                              _rope(kc_ref[:, sl].astype(F32), cq, sq)], axis=0)
        k2t = k2.T.astype(BF16)
        v2 = jnp.concatenate([vp_ref[:, sl], vc_ref[:, sl]], axis=0)
        heads = [hk * group + g for g in range(group)]
        qs = jnp.concatenate(
            [_rope(q_ref[:, h * HEAD_DIM:(h + 1) * HEAD_DIM].astype(F32), cq, sq).astype(BF16)
             for h in heads], axis=0)
        s_all = _dot(qs, k2t) * scale
        ps, dens = [], []
        for g, h in enumerate(heads):
            s = jnp.where(mask, s_all[g * BLOCK:(g + 1) * BLOCK], NEG)
            sink = sink_ref[h]
            m = jnp.maximum(jnp.max(s, axis=-1, keepdims=True), sink)
            p = jnp.exp(s - m)
            dens.append(jnp.sum(p, axis=-1, keepdims=True) + jnp.exp(sink - m))
            ps.append(p.astype(BF16))
        o_all = _dot(jnp.concatenate(ps, axis=0), v2)
        for g, h in enumerate(heads):
            o_ref[:, h * HEAD_DIM:(h + 1) * HEAD_DIM] = (
                o_all[g * BLOCK:(g + 1) * BLOCK] / dens[g]).astype(o_ref.dtype)


def _swa(proj, cos2, sin2, sinks, batch, seq, n_heads, n_kv):
    nt = proj.shape[0]
    nq = seq // BLOCK
    qw, kw = n_heads * HEAD_DIM, n_kv * HEAD_DIM
    kcol = n_heads // n_kv
    vcol = kcol + 1
    cur = lambda b, i: b * nq + i
    prev = lambda b, i: b * nq + jnp.maximum(i - 1, 0)
    return pl.pallas_call(
        functools.partial(_swa_kernel, n_heads=n_heads, n_kv=n_kv, scale=HEAD_DIM ** -0.5),
        grid=(batch, nq),
        in_specs=[
            pl.BlockSpec(memory_space=pltpu.SMEM),
            pl.BlockSpec((BLOCK, qw), lambda b, i: (cur(b, i), 0)),
            pl.BlockSpec((BLOCK, kw), lambda b, i: (cur(b, i), kcol)),
            pl.BlockSpec((BLOCK, kw), lambda b, i: (prev(b, i), kcol)),
            pl.BlockSpec((BLOCK, kw), lambda b, i: (cur(b, i), vcol)),
            pl.BlockSpec((BLOCK, kw), lambda b, i: (prev(b, i), vcol)),
            pl.BlockSpec((BLOCK, HEAD_DIM), lambda b, i: (cur(b, i), 0)),
            pl.BlockSpec((BLOCK, HEAD_DIM), lambda b, i: (cur(b, i), 0)),
            pl.BlockSpec((BLOCK, HEAD_DIM), lambda b, i: (prev(b, i), 0)),
            pl.BlockSpec((BLOCK, HEAD_DIM), lambda b, i: (prev(b, i), 0)),
        ],
        out_specs=pl.BlockSpec((BLOCK, qw), lambda b, i: (cur(b, i), 0)),
        out_shape=jax.ShapeDtypeStruct((nt, qw), BF16),
        compiler_params=_params("arbitrary", "arbitrary"),
        name="swa_attn",
    )(sinks, proj, proj, proj, proj, proj, cos2, sin2, cos2, sin2)


SB_TILE = 256
SB_HEADS_PER_STEP = 4
SB_DROP_CAP = 104.0


def _sb_kernel(q_ref, kt_ref, v_ref, o_ref, acc_ref, run_ref, *, scale, tile, hp):
    i = pl.program_id(2)
    row = lax.broadcasted_iota(jnp.int32, (tile, tile), 0)
    col = lax.broadcasted_iota(jnp.int32, (tile, tile), 1)
    strict = col < row
    rr = lax.broadcasted_iota(jnp.int32, (tile, tile + HEAD_DIM), 0)
    cc = lax.broadcasted_iota(jnp.int32, (tile, tile + HEAD_DIM), 1)
    cum = ((rr > cc) | (cc >= tile)).astype(BF16)

    def head_step(h, j, diag):
        hs = slice(h * HEAD_DIM, (h + 1) * HEAD_DIM)
        off = pl.multiple_of(j * tile, tile)
        v = v_ref[pl.ds(off, tile), hs]
        z = _dot(q_ref[:, hs], kt_ref[0, h, j]) * scale
        soft = jnp.log(1.0 + jnp.exp(-jnp.abs(z)))
        log_beta = jnp.minimum(z, 0.0) - soft
        drop = jnp.maximum(z, 0.0) + soft
        if diag:
            drop = jnp.where(strict, drop, 0.0)
        hi = drop.astype(BF16)
        lo = (drop - hi.astype(F32)).astype(BF16)
        sums = _dot(hi, cum) + _dot(lo, cum)
        after = sums[:, :tile]
        if not diag:
            run = run_ref[h]
            after = after + jnp.concatenate([run] * (tile // HEAD_DIM), axis=1)
        w = jnp.exp(log_beta - after)
        if diag:
            w = jnp.where(strict, w, 0.0)
        pv = _dot(w.astype(BF16), v)
        if diag:
            acc_ref[h] = pv
            run = sums[:, tile:]
        else:
            acc_ref[h] += pv
            run = run + sums[:, tile:]
        run_ref[h] = run
        return jnp.min(run)

    def step(j, diag):
        least = head_step(0, j, diag)
        for h in range(1, hp):
            least = jnp.minimum(least, head_step(h, j, diag))
        return least

    def cond(st):
        return (st[0] >= 0) & (st[1] < SB_DROP_CAP)

    def body(st):
        return st[0] - 1, step(st[0], False)

    lax.while_loop(cond, body, (i - 1, step(i, True)))
    for h in range(hp):
        o_ref[:, h * HEAD_DIM:(h + 1) * HEAD_DIM] = acc_ref[h].astype(o_ref.dtype)


def _sb(proj, kt, batch, seq, n_heads, qcol, vcol):
    nt = proj.shape[0]
    tile = min(SB_TILE, seq)
    nq = seq // tile
    hp = math.gcd(math.gcd(qcol, vcol), math.gcd(n_heads, SB_HEADS_PER_STEP))
    w = hp * HEAD_DIM
    return pl.pallas_call(
        functools.partial(_sb_kernel, scale=HEAD_DIM ** -0.5, tile=tile, hp=hp),
        grid=(batch, n_heads // hp, nq),
        in_specs=[
            pl.BlockSpec((tile, w), lambda b, h, i: (b * nq + i, qcol // hp + h)),
            pl.BlockSpec((1, hp, nq, HEAD_DIM, tile), lambda b, h, i: (b, h, 0, 0, 0)),
            pl.BlockSpec((seq, w), lambda b, h, i: (b, vcol // hp + h)),
        ],
        out_specs=pl.BlockSpec((tile, w), lambda b, h, i: (b * nq + i, h)),
        out_shape=jax.ShapeDtypeStruct((nt, n_heads * HEAD_DIM), BF16),
        scratch_shapes=[pltpu.VMEM((hp, tile, HEAD_DIM), F32), pltpu.VMEM((hp, tile, HEAD_DIM), F32)],
        compiler_params=_params("arbitrary", "arbitrary", "arbitrary"),
        name="sb_attn",
    )(proj, kt, proj)


def _rope_heads_t_kernel(x_ref, c_ref, s_ref, o_ref, *, n_heads, rotate, kw, cps):
    for j in range(cps):
        rows = slice(j * kw, (j + 1) * kw)
        c, s = c_ref[rows, :], s_ref[rows, :]
        for h in range(n_heads):
            x = x_ref[rows, h * HEAD_DIM:(h + 1) * HEAD_DIM].astype(F32)
            if rotate:
                x = _rope(x, c, s)
            o_ref[0, h, j] = x.T.astype(o_ref.dtype)


def _heads_t(x, first_head, n_heads, cos2, sin2, batch, seq, kw, rotate, name):
    hg = math.gcd(first_head, n_heads)
    n_chunks = seq // kw
    cps = math.gcd(n_chunks, max(1, 1024 // kw))
    steps = n_chunks // cps
    return pl.pallas_call(
        functools.partial(_rope_heads_t_kernel, n_heads=hg, rotate=rotate, kw=kw, cps=cps),
        grid=(batch, steps, n_heads // hg),
        in_specs=[
            pl.BlockSpec((cps * kw, hg * HEAD_DIM), lambda b, c, g: (b * steps + c, first_head // hg + g)),
            pl.BlockSpec((cps * kw, HEAD_DIM), lambda b, c, g: (b * steps + c, 0)),
            pl.BlockSpec((cps * kw, HEAD_DIM), lambda b, c, g: (b * steps + c, 0)),
        ],
        out_specs=pl.BlockSpec((1, hg, cps, HEAD_DIM, kw), lambda b, c, g: (b, g, c, 0, 0)),
        out_shape=jax.ShapeDtypeStruct((batch, n_heads, n_chunks, HEAD_DIM, kw), BF16),
        compiler_params=_params("arbitrary", "arbitrary", "arbitrary"),
        name=name,
    )(x, cos2, sin2)


def _idx_key_kernel(t_ref, g_ref, c_ref, s_ref, o_ref):
    x = t_ref[:, 0:HEAD_DIM]
    y = x * lax.rsqrt(jnp.mean(x * x, axis=-1, keepdims=True) + NORM_EPS) * g_ref[...]
    o_ref[...] = _rope(y, c_ref[...], s_ref[...]).astype(o_ref.dtype)


def _idx_key(tail, g, cos2, sin2, tr=512):
    nt, tw = tail.shape
    tr = min(tr, nt)
    return pl.pallas_call(
        _idx_key_kernel,
        grid=(nt // tr,),
        in_specs=[
            pl.BlockSpec((tr, tw), lambda i: (i, 0)),
            pl.BlockSpec((1, HEAD_DIM), lambda i: (0, 0)),
            pl.BlockSpec((tr, HEAD_DIM), lambda i: (i, 0)),
            pl.BlockSpec((tr, HEAD_DIM), lambda i: (i, 0)),
        ],
        out_specs=pl.BlockSpec((tr, HEAD_DIM), lambda i: (i, 0)),
        out_shape=jax.ShapeDtypeStruct((nt, HEAD_DIM), BF16),
        compiler_params=_params("arbitrary"),
        name="idx_key",
    )(tail, g.reshape(1, HEAD_DIM), cos2, sin2)


def _idx_kernel(qi_ref, kir_ref, tail_ref, c_ref, s_ref, o_ref, qr_ref, wb_ref, sk_ref, run_ref,
                *, n_ih, kw, topk, idx_scale, n_chunks):
    i = pl.program_id(1)
    cq, sq = c_ref[...], s_ref[...]
    wi = tail_ref[:, HEAD_DIM:2 * HEAD_DIM]
    for h in range(n_ih):
        hs = slice(h * BLOCK, (h + 1) * BLOCK)
        qr_ref[hs, :] = _rope(qi_ref[:, hs].astype(F32), cq, sq).astype(BF16)
        wb_ref[h] = jnp.broadcast_to(wi[:, h:h + 1], (BLOCK, HEAD_DIM))
    nc = ((i + 1) * BLOCK + kw - 1) // kw
    n_lt = kw // HEAD_DIM
    qidx = i * BLOCK + lax.broadcasted_iota(jnp.int32, (BLOCK, kw), 0)
    kloc = lax.broadcasted_iota(jnp.int32, (BLOCK, kw), 1)

    def score_chunk(c, _):
        off = pl.multiple_of(c * kw, kw)
        kc = kir_ref[pl.ds(off, kw), :]
        lg = _dot_nt(qr_ref[...], kc)
        accs = [jnp.zeros((BLOCK, HEAD_DIM), F32) for _ in range(n_lt)]
        for h in range(n_ih):
            w = wb_ref[h]
            for t in range(n_lt):
                tile = lg[h * BLOCK:(h + 1) * BLOCK, t * HEAD_DIM:(t + 1) * HEAD_DIM]
                accs[t] = accs[t] + jnp.maximum(tile, 0.0) * w
        acc = jnp.concatenate(accs, axis=1)
        score = jnp.where(c * kw + kloc <= qidx, acc * idx_scale, -jnp.inf)
        bits = lax.bitcast_convert_type(score, jnp.int32)
        bits = jnp.where(bits == INT_MIN, 0, bits)
        sk_ref[c] = jnp.where(bits < 0, bits ^ 0x7FFFFFFF, bits)
        return 0

    def score_pair(t, _):
        score_chunk(2 * t, 0)
        score_chunk(2 * t + 1, 0)
        return 0

    lax.fori_loop(0, nc // 2, score_pair, 0)

    @pl.when(nc % 2 == 1)
    def _():
        score_chunk(nc - 1, 0)

    def count_ge(cand):
        cb = jnp.broadcast_to(cand, (BLOCK, HEAD_DIM))

        def body(c, acc):
            x = sk_ref[c]
            for t in range(n_lt):
                acc = acc + jnp.where(x[:, t * HEAD_DIM:(t + 1) * HEAD_DIM] >= cb, 1.0, 0.0)
            return acc

        acc = lax.fori_loop(0, nc, body, jnp.zeros((BLOCK, HEAD_DIM), F32))
        return jnp.sum(acc, axis=-1, keepdims=True)

    zero = jnp.zeros((BLOCK, 1), jnp.int32)
    thr = jnp.where(count_ge(zero) >= topk, zero, INT_MIN)

    def bit_step(t, thr):
        cand = thr | (jnp.int32(1) << (30 - t))
        return jnp.where(count_ge(cand) >= topk, cand, thr)

    thr = lax.fori_loop(0, 31, bit_step, thr)
    tb = jnp.broadcast_to(thr, (BLOCK, kw))
    has_ties = jnp.max(count_ge(thr)) > topk

    @pl.when(jnp.logical_not(has_ties))
    def _():
        for c in range(n_chunks):
            @pl.when(c < nc)
            def _():
                sel = (sk_ref[c] >= tb) & (c * kw + kloc <= qidx)
                o_ref[0, c] = jnp.where(sel, 0.0, NEG).astype(o_ref.dtype)

    @pl.when(has_ties)
    def _():
        need = jnp.broadcast_to(topk - count_ge(thr + 1), (BLOCK, kw)).astype(F32)
        rr = lax.broadcasted_iota(jnp.int32, (kw, kw + HEAD_DIM), 0)
        cc = lax.broadcasted_iota(jnp.int32, (kw, kw + HEAD_DIM), 1)
        before = ((rr < cc) | (cc >= kw)).astype(BF16)
        run_ref[...] = jnp.zeros((BLOCK, HEAD_DIM), F32)
        for c in range(n_chunks):
            @pl.when(c < nc)
            def _():
                x = sk_ref[c]
                causal = c * kw + kloc <= qidx
                tied = (x == tb) & causal
                sums = _dot(jnp.where(tied, 1.0, 0.0).astype(BF16), before)
                run = run_ref[...]
                rank = sums[:, :kw] + jnp.concatenate([run] * n_lt, axis=1)
                sel = ((x > tb) & causal) | (tied & (rank < need))
                o_ref[0, c] = jnp.where(sel, 0.0, NEG).astype(o_ref.dtype)
                run_ref[...] = run + sums[:, kw:]

    for c in range(n_chunks):
        @pl.when(c >= nc)
        def _():
            o_ref[0, c] = jnp.full((BLOCK, kw), NEG, o_ref.dtype)


def _indexer(qi, kir, tail, cos2, sin2, batch, seq, n_ih, topk, kw):
    nt = qi.shape[0]
    nq = seq // BLOCK
    n_chunks = seq // kw
    idx_scale = (n_ih ** -0.5) * (HEAD_DIM ** -0.5)
    return pl.pallas_call(
        functools.partial(_idx_kernel, n_ih=n_ih, kw=kw, topk=topk, idx_scale=idx_scale,
                          n_chunks=n_chunks),
        grid=(batch, nq),
        in_specs=[
            pl.BlockSpec((BLOCK, n_ih * HEAD_DIM), lambda b, i: (b * nq + i, 0)),
            pl.BlockSpec((seq, HEAD_DIM), lambda b, i: (b, 0)),
            pl.BlockSpec((BLOCK, 2 * HEAD_DIM), lambda b, i: (b * nq + i, 0)),
            pl.BlockSpec((BLOCK, HEAD_DIM), lambda b, i: (b * nq + i, 0)),
            pl.BlockSpec((BLOCK, HEAD_DIM), lambda b, i: (b * nq + i, 0)),
        ],
        out_specs=pl.BlockSpec((1, n_chunks, BLOCK, kw), lambda b, i: (b * nq + i, 0, 0, 0)),
        out_shape=jax.ShapeDtypeStruct((batch * nq, n_chunks, BLOCK, kw), BF16),
        scratch_shapes=[
            pltpu.VMEM((n_ih * BLOCK, HEAD_DIM), BF16),
            pltpu.VMEM((n_ih, BLOCK, HEAD_DIM), F32),
            pltpu.VMEM((n_chunks, BLOCK, kw), jnp.int32),
            pltpu.VMEM((BLOCK, HEAD_DIM), F32),
        ],
        compiler_params=_params("arbitrary", "arbitrary"),
        name="dsa_indexer",
    )(qi, kir, tail, cos2, sin2)


LOG2E = 1.4426950408889634


def _dsa_kernel(q_ref, kt_ref, v_ref, bias_ref, c_ref, s_ref, o_ref, qs_ref, m_ref, l_ref, acc_ref, sc_ref,
                *, group, qb, kw, scale):
    i = pl.program_id(2)
    n_rg = group * qb
    for g in range(group):
        for t in range(qb):
            r = g * qb + t
            rows = slice(t * BLOCK, (t + 1) * BLOCK)
            qg = _rope(q_ref[rows, g * HEAD_DIM:(g + 1) * HEAD_DIM].astype(F32), c_ref[rows, :], s_ref[rows, :])
            qs_ref[r * BLOCK:(r + 1) * BLOCK, :] = (qg * (scale * LOG2E)).astype(BF16)
            m_ref[r] = jnp.full((BLOCK, HEAD_DIM), NEG, F32)
            l_ref[r] = jnp.zeros((BLOCK, HEAD_DIM), F32)
            acc_ref[r] = jnp.zeros((BLOCK, HEAD_DIM), F32)
    nc = ((i + 1) * qb * BLOCK + kw - 1) // kw
    ones = jnp.ones((kw, HEAD_DIM), BF16)

    def scores(c):
        return _dot(qs_ref[...], kt_ref[0, 0, c])

    def half(c, src, dst):
        s_all = sc_ref[src]
        sc_ref[dst] = scores(jnp.minimum(c + 1, kt_ref.shape[2] - 1))
        off = pl.multiple_of(c * kw, kw)
        vx = jnp.concatenate([v_ref[pl.ds(off, kw), :], ones], axis=1)
        bs = [bias_ref[t, c].astype(F32) for t in range(qb)]
        ps, alphas = [], []
        for r in range(n_rg):
            s = s_all[r * BLOCK:(r + 1) * BLOCK] + bs[r % qb]
            m_old = m_ref[r]
            m_new = jnp.maximum(m_old, jnp.max(s, axis=-1, keepdims=True))
            alphas.append(jnp.exp2(m_old - m_new))
            ps.append(jnp.exp2(s - jnp.concatenate([m_new] * (kw // HEAD_DIM), axis=1)).astype(BF16))
            m_ref[r] = m_new
        pv = _dot(jnp.concatenate(ps, axis=0), vx)
        for r in range(n_rg):
            pr = pv[r * BLOCK:(r + 1) * BLOCK]
            acc_ref[r] = acc_ref[r] * alphas[r] + pr[:, :HEAD_DIM]
            l_ref[r] = l_ref[r] * alphas[r] + pr[:, HEAD_DIM:]

    n_pairs = (nc + 1) // 2

    def two_pairs(t, _):
        for u in range(4):
            half(4 * t + u, u % 2, 1 - u % 2)
        return 0

    sc_ref[0] = scores(0)
    lax.fori_loop(0, n_pairs // 2, two_pairs, 0)

    @pl.when(n_pairs % 2 == 1)
    def _():
        half(2 * n_pairs - 2, 0, 1)
        half(2 * n_pairs - 1, 1, 0)
    for g in range(group):
        for t in range(qb):
            r = g * qb + t
            o_ref[t * BLOCK:(t + 1) * BLOCK, g * HEAD_DIM:(g + 1) * HEAD_DIM] = (
                acc_ref[r] / l_ref[r]).astype(o_ref.dtype)


DSA_Q_BLOCKS = 1


def _dsa(qkv, kr, bias, cos2, sin2, batch, seq, n_heads, n_kv, kw):
    nt = qkv.shape[0]
    qb = DSA_Q_BLOCKS if (seq // BLOCK) % DSA_Q_BLOCKS == 0 else 1
    tq = qb * BLOCK
    nq = seq // tq
    group = n_heads // n_kv
    n_rg = group * qb
    n_chunks = seq // kw
    assert n_chunks % 2 == 0
    vcol = n_heads + n_kv
    return pl.pallas_call(
        functools.partial(_dsa_kernel, group=group, qb=qb, kw=kw, scale=HEAD_DIM ** -0.5),
        grid=(batch, n_kv, nq),
        in_specs=[
            pl.BlockSpec((tq, group * HEAD_DIM), lambda b, h, i: (b * nq + i, h)),
            pl.BlockSpec((1, 1, n_chunks, HEAD_DIM, kw), lambda b, h, i: (b, h, 0, 0, 0)),
            pl.BlockSpec((seq, HEAD_DIM), lambda b, h, i: (b, vcol + h)),
            pl.BlockSpec((qb, n_chunks, BLOCK, kw), lambda b, h, i: (b * nq + i, 0, 0, 0)),
            pl.BlockSpec((tq, HEAD_DIM), lambda b, h, i: (b * nq + i, 0)),
            pl.BlockSpec((tq, HEAD_DIM), lambda b, h, i: (b * nq + i, 0)),
        ],
        out_specs=pl.BlockSpec((tq, group * HEAD_DIM), lambda b, h, i: (b * nq + i, h)),
        out_shape=jax.ShapeDtypeStruct((nt, n_heads * HEAD_DIM), BF16),
        scratch_shapes=[
            pltpu.VMEM((n_rg * BLOCK, HEAD_DIM), BF16),
            pltpu.VMEM((n_rg, BLOCK, HEAD_DIM), F32),
            pltpu.VMEM((n_rg, BLOCK, HEAD_DIM), F32),
            pltpu.VMEM((n_rg, BLOCK, HEAD_DIM), F32),
            pltpu.VMEM((2, n_rg * BLOCK, kw), F32),
        ],
        compiler_params=_params("arbitrary", "arbitrary", "arbitrary"),
        name="dsa_attn",
    )(qkv, kr, qkv, bias, cos2, sin2)


def _rope_tables(positions):
    inv_freq = ROPE_THETA ** (-jnp.arange(0, HEAD_DIM, 2, dtype=F32) / HEAD_DIM)
    ang = positions.astype(F32).reshape(-1)[:, None] * inv_freq
    cos, sin = jnp.cos(ang), jnp.sin(ang)
    return jnp.concatenate([cos, cos], axis=-1), jnp.concatenate([-sin, sin], axis=-1)


def _pick_tile(n, prefs):
    for t in prefs:
        if n % t == 0:
            return t
    return n


def kernel(x, c, positions, norm1_g, norm2_g, ada_w, ada_b, even_w_in, even_sinks, even_w_o,
           odd_w_in, odd_idx_k_g, odd_w_o, ffn_w_gate, ffn_w_up, ffn_conv_w, ffn_conv_b,
           ffn_w_down, final_g):
    batch, seq, d = x.shape
    depth = norm1_g.shape[0]
    nt = batch * seq
    a_heads = even_sinks.shape[1]
    a_kv = max(1, a_heads // 8)
    b_heads = even_w_o.shape[1] // HEAD_DIM - a_heads
    c_heads = odd_w_o.shape[1] // HEAD_DIM
    c_kv = c_heads // 4
    odd_in = odd_w_in.shape[2]
    n_ih = (odd_in - (c_heads + 2 * c_kv) * HEAD_DIM - HEAD_DIM) // (HEAD_DIM + 1)
    topk = min(TOPK_MAX, seq // 4)
    kw = min(512, seq)

    cos2, sin2 = _rope_tables(positions)
    c_pad = jnp.zeros((8, d), F32).at[:batch].set(c)
    mods = _ada(c_pad, ada_w, ada_b, tn=_pick_tile(6 * d, (512, 256, 128)))[:, :batch]
    mods = mods.reshape(depth, batch, 6, 1, d)

    tm = 1024
    w_down = ffn_w_down.astype(BF16)
    h = x.reshape(nt, d)
    for layer in range(depth):
        shift1, scale1, gate1, shift2, scale2, gate2 = [mods[layer, :, k] for k in range(6)]
        j = layer // 2
        hn = _norm_mod(h, norm1_g[layer], scale1, shift1, seq)
        if layer % 2 == 0:
            w_in = even_w_in
            n_in = w_in.shape[2]
            proj = _matmul([hn], w_in, n_in, 0, tm, _pick_tile(n_in, (512, 256, 128)), BF16, "in_proj_even",
                           layer=j)
            oa = _swa(proj, cos2, sin2, even_sinks[j], batch, seq, a_heads, a_kv)
            qcol = a_heads + 2 * a_kv
            kt = _heads_t(proj, qcol + b_heads, b_heads, cos2, sin2, batch, seq, min(SB_TILE, seq), False,
                          "sb_kt")
            ob = _sb(proj, kt, batch, seq, b_heads, qcol, qcol + 2 * b_heads)
            w_o = even_w_o
            if a_heads == b_heads:
                parts = [oa, ob]
            else:
                parts = [jnp.concatenate([oa, ob], axis=-1)]
            h = _matmul(parts, w_o, d, 0, tm, _pick_tile(d, (512, 256, 128)), F32, "out_proj_even",
                        res=h, gate=gate1, seq=seq, layer=j)
        else:
            w_in = odd_w_in.astype(BF16)
            n_qkv = (c_heads + 2 * c_kv) * HEAD_DIM
            n_qi = n_ih * HEAD_DIM
            qkv = _matmul([hn], w_in, n_qkv, 0, tm, _pick_tile(n_qkv, (512, 256, 128)), BF16, "in_proj_qkv",
                          layer=j)
            tn_qi = _pick_tile(n_qi, [t for t in (512, 256, 128) if n_qkv % t == 0])
            qi = _matmul([hn], w_in, n_qi, n_qkv, tm, tn_qi, BF16, "in_proj_qi", layer=j)
            w_tail = jnp.zeros((1, d, 2 * HEAD_DIM), BF16).at[0, :, :HEAD_DIM + n_ih].set(
                w_in[j, :, n_qkv + n_qi:].astype(BF16))
            tail = _matmul([hn], w_tail, 2 * HEAD_DIM, 0, tm, 2 * HEAD_DIM, F32, "in_proj_idx")
            kr = _heads_t(qkv, c_heads, c_kv, cos2, sin2, batch, seq, kw, True, "rope_k")
            kir = _idx_key(tail, odd_idx_k_g[j], cos2, sin2)
            bias = _indexer(qi, kir, tail, cos2, sin2, batch, seq, n_ih, topk, kw)
            o = _dsa(qkv, kr, bias, cos2, sin2, batch, seq, c_heads, c_kv, kw)
            w_o = odd_w_o
            h = _matmul([o], w_o, d, 0, tm, _pick_tile(d, (512, 256, 128)), F32, "out_proj_odd",
                        res=h, gate=gate1, seq=seq, layer=j)
        hn = _norm_mod(h, norm2_g[layer], scale2, shift2, seq)
        a = _ffn_up(hn, ffn_w_gate, ffn_w_up, layer, ffn_conv_w[layer], ffn_conv_b[layer], seq)
        h = _matmul([a], w_down, d, 0, 1024, 256, F32, "ffn_down", res=h, gate=gate2, seq=seq, layer=layer,
                    single_buffer_x=True)
    return _norm_plain(h, final_g).reshape(batch, seq, d)
```
